```python
import math
import jax
import jax.numpy as jnp
from jax import lax
import numpy as np

D_MODEL = 1024
BATCH = 8
SEQ = 4096
DEPTH = 1

PLE_DIM = 256
GMLP_WIDTH = 1024
GMLP_GROUPS = 8
GMLP_CHUNK = 128
MOBA_HEADS = 8
MOBA_HEAD_DIM = 128
MOBA_WIDTH = MOBA_HEADS * MOBA_HEAD_DIM
MOBA_BLOCK = 256
MOBA_TOPK = 3
MOBA_QCHUNK = 32
REL_BUCKETS = 32
REL_MAX_DISTANCE = 1024
PEER_HEADS = 8
PEER_N_KEYS = 128
PEER_N_EXPERTS = PEER_N_KEYS * PEER_N_KEYS
PEER_TOPK = 16
PEER_KEY_DIM = 256
PEER_KEY_HALF = PEER_KEY_DIM // 2
PEER_TOK_CHUNK = 128
IN_WIDTHS = (GMLP_WIDTH, GMLP_WIDTH, MOBA_WIDTH, MOBA_WIDTH, MOBA_WIDTH, D_MODEL, D_MODEL)
IN_TOTAL = sum(IN_WIDTHS)
LN_EPS = 1e-5
NEG_INF = -1e30

kernel_name = 'hybrid_gmlp_moba_peer_block'


def layer_norm(x, g, b):
    xf = x.astype(jnp.float32)
    mu = jnp.mean(xf, axis=-1, keepdims=True)
    var = jnp.mean(jnp.square(xf - mu), axis=-1, keepdims=True)
    y = (xf - mu) * lax.rsqrt(var + LN_EPS) * g.astype(jnp.float32) + b.astype(jnp.float32)
    return y.astype(x.dtype)


def t5_bucket(dist):
    n = jnp.maximum(dist, 0)
    max_exact = REL_BUCKETS // 2
    nf = jnp.maximum(n, max_exact).astype(jnp.float32)
    large = max_exact + (jnp.log(nf / max_exact) / math.log(REL_MAX_DISTANCE / max_exact)
                         * (REL_BUCKETS - max_exact)).astype(jnp.int32)
    large = jnp.minimum(large, REL_BUCKETS - 1)
    return jnp.where(n < max_exact, n, large)


def gmlp_spatial_gating(u, v, ln_g, ln_b, w_s, b_s):
    bsz, seq, width = v.shape
    v = layer_norm(v, ln_g, ln_b)
    vc = v.reshape(bsz, seq // GMLP_CHUNK, GMLP_CHUNK, GMLP_GROUPS, width // GMLP_GROUPS)
    w_causal = jnp.tril(w_s)
    s = jnp.einsum('gts,bnsgc->bntgc', w_causal, vc) + b_s.T[None, None, :, :, None]
    return u * s.reshape(bsz, seq, width)


def moba_attention(q, k, v, rel_bias):
    bsz, seq, nh, hd = q.shape
    nb = -(-seq // MOBA_BLOCK)
    s_pad = nb * MOBA_BLOCK
    pad = ((0, 0), (0, s_pad - seq), (0, 0), (0, 0))
    q, k, v = (jnp.pad(t, pad).transpose(0, 2, 1, 3) for t in (q, k, v))
    kb = k.reshape(bsz, nh, nb, MOBA_BLOCK, hd)
    vb = v.reshape(bsz, nh, nb, MOBA_BLOCK, hd)
    scale = hd ** -0.5

    k_mean = jnp.mean(kb.astype(jnp.float32), axis=3).astype(q.dtype)
    gate = jnp.einsum('bhtd,bhnd->bhtn', q, k_mean).astype(jnp.float32)
    q_blk = jnp.arange(s_pad) // MOBA_BLOCK
    past = jnp.arange(nb)[None, :] < q_blk[:, None]
    gate = jnp.where(past, gate, NEG_INF)
    n_sel = min(MOBA_TOPK, nb)
    _, sel = lax.top_k(gate, n_sel)
    sel_valid = jnp.arange(n_sel)[None, :] < q_blk[:, None]

    n_chunks = s_pad // MOBA_QCHUNK
    q_c = q.reshape(bsz, nh, n_chunks, MOBA_QCHUNK, hd).transpose(2, 0, 1, 3, 4)
    sel_c = sel.reshape(bsz, nh, n_chunks, MOBA_QCHUNK, n_sel).transpose(2, 0, 1, 3, 4)
    valid_c = sel_valid.reshape(n_chunks, MOBA_QCHUNK, n_sel)
    b_idx = jnp.arange(bsz)[:, None, None]
    h_idx = jnp.arange(nh)[None, :, None]
    bias_h = rel_bias.T.astype(jnp.float32)
    offs = jnp.arange(MOBA_BLOCK)

    def chunk_attend(args):
        c, qc, selc, validc = args
        t = c * MOBA_QCHUNK + jnp.arange(MOBA_QCHUNK)
        own = (c * MOBA_QCHUNK) // MOBA_BLOCK
        k_own = lax.dynamic_index_in_dim(kb, own, axis=2, keepdims=False)
        v_own = lax.dynamic_index_in_dim(vb, own, axis=2, keepdims=False)
        d_own = t[:, None] - (own * MOBA_BLOCK + offs)[None, :]
        l_own = (jnp.einsum('bhqd,bhkd->bhqk', qc, k_own).astype(jnp.float32) * scale
                 + bias_h[:, t5_bucket(d_own)])
        logits = [jnp.where(d_own >= 0, l_own, NEG_INF)]
        for r in range(n_sel):
            idx = selc[..., r]
            k_g = kb[b_idx, h_idx, idx]
            d_r = t[:, None] - (idx[..., None] * MOBA_BLOCK + offs)
            l_r = (jnp.einsum('bhqd,bhqkd->bhqk', qc, k_g).astype(jnp.float32) * scale
                   + bias_h[h_idx[..., None], t5_bucket(d_r)])
            logits.append(jnp.where(validc[None, None, :, r, None], l_r, NEG_INF))
        w = jax.nn.softmax(jnp.concatenate(logits, axis=-1), axis=-1).astype(qc.dtype)
        out = jnp.einsum('bhqk,bhkd->bhqd', w[..., :MOBA_BLOCK], v_own)
        for r in range(n_sel):
            v_g = vb[b_idx, h_idx, selc[..., r]]
            w_r = w[..., (r + 1) * MOBA_BLOCK:(r + 2) * MOBA_BLOCK]
            out = out + jnp.einsum('bhqk,bhqkd->bhqd', w_r, v_g)
        return out

    out = lax.map(chunk_attend, (jnp.arange(n_chunks), q_c, sel_c, valid_c))
    out = out.transpose(1, 0, 3, 2, 4).reshape(bsz, s_pad, nh * hd)
    return out[:, :seq]


def peer_ffn(h, w_q, sub_keys, u_tab, v_tab):
    bsz, seq, d = h.shape
    n_tok = bsz * seq
    xt = h.reshape(n_tok, d)
    q = (xt @ w_q).reshape(n_tok, PEER_HEADS, 2, PEER_KEY_HALF)
    s = jnp.einsum('thpk,pnk->thpn', q, sub_keys).astype(jnp.float32)
    top_s, top_i = lax.top_k(s, PEER_TOPK)
    n_cand = PEER_TOPK * PEER_TOPK
    cand_s = (top_s[:, :, 0, :, None] + top_s[:, :, 1, None, :]).reshape(n_tok, PEER_HEADS, n_cand)
    cand_i = (top_i[:, :, 0, :, None] * PEER_N_KEYS + top_i[:, :, 1, None, :]).reshape(n_tok, PEER_HEADS, n_cand)
    best_s, best_pos = lax.top_k(cand_s, PEER_TOPK)
    experts = jnp.take_along_axis(cand_i, best_pos, axis=-1)
    gates = jax.nn.softmax(best_s, axis=-1).astype(h.dtype)
    n_chunks = n_tok // PEER_TOK_CHUNK

    def chunk_ffn(args):
        xc, ec, gc = args
        hid = jax.nn.gelu(jnp.einsum('chkd,cd->chk', u_tab[ec], xc), approximate=False)
        return jnp.einsum('chk,chkd->cd', gc * hid, v_tab[ec])

    y = lax.map(chunk_ffn, (xt.reshape(n_chunks, PEER_TOK_CHUNK, d),
                            experts.reshape(n_chunks, PEER_TOK_CHUNK, PEER_HEADS, PEER_TOPK),
                            gates.reshape(n_chunks, PEER_TOK_CHUNK, PEER_HEADS, PEER_TOPK)))
    return y.reshape(bsz, seq, d)


def setup_inputs(seed: int = 0) -> dict:
    key = jax.random.key(seed)
    ks = jax.random.split(key, 24)
    f32 = jnp.float32
    beta = (8.0 * DEPTH) ** -0.25

    def nrm(k, shape, s):
        return jax.random.normal(k, shape, f32) * s

    v_off = 2 * GMLP_WIDTH + 2 * MOBA_WIDTH
    col_scale = jnp.ones((IN_TOTAL,), f32).at[v_off:v_off + MOBA_WIDTH].set(beta)
    return {
        'x': nrm(ks[0], (BATCH, SEQ, D_MODEL), 1.0),
        'p': nrm(ks[1], (DEPTH, BATCH, SEQ, PLE_DIM), 1.0),
        'w_in': nrm(ks[2], (DEPTH, D_MODEL, IN_TOTAL), D_MODEL ** -0.5) * col_scale,
        'b_in': nrm(ks[3], (DEPTH, IN_TOTAL), 0.02),
        'gmlp_ln_g': 1.0 + nrm(ks[4], (DEPTH, GMLP_WIDTH), 0.02),
        'gmlp_ln_b': nrm(ks[5], (DEPTH, GMLP_WIDTH), 0.02),
        'gmlp_w_s': nrm(ks[6], (DEPTH, GMLP_GROUPS, GMLP_CHUNK, GMLP_CHUNK), GMLP_CHUNK ** -0.5),
        'gmlp_b_s': 1.0 + nrm(ks[7], (DEPTH, GMLP_GROUPS, GMLP_CHUNK), 0.02),
        'w_proj_a': nrm(ks[8], (DEPTH, GMLP_WIDTH, D_MODEL), beta * GMLP_WIDTH ** -0.5),
        'w_proj_b': nrm(ks[9], (DEPTH, MOBA_WIDTH, D_MODEL), beta * MOBA_WIDTH ** -0.5),
        'w_out': nrm(ks[10], (DEPTH, D_MODEL, D_MODEL), beta * D_MODEL ** -0.5),
        'ln1_g': 1.0 + nrm(ks[11], (DEPTH, D_MODEL), 0.02),
        'ln1_b': nrm(ks[12], (DEPTH, D_MODEL), 0.02),
        'rel_bias': nrm(ks[13], (REL_BUCKETS, MOBA_HEADS), 0.5),
        'peer_w_q': nrm(ks[14], (DEPTH, D_MODEL, PEER_HEADS * PEER_KEY_DIM), D_MODEL ** -0.5),
        'peer_sub_keys': nrm(ks[15], (DEPTH, 2, PEER_N_KEYS, PEER_KEY_HALF), PEER_KEY_HALF ** -0.5),
        'peer_u': nrm(ks[16], (DEPTH, PEER_N_EXPERTS, D_MODEL), D_MODEL ** -0.5),
        'peer_v': nrm(ks[17], (DEPTH, PEER_N_EXPERTS, D_MODEL), beta * PEER_HEADS ** -0.5),
        'ple_w_proj': nrm(ks[18], (DEPTH, PLE_DIM, D_MODEL), beta * PLE_DIM ** -0.5),
        'ple_w_gate': nrm(ks[19], (DEPTH, D_MODEL, D_MODEL), D_MODEL ** -0.5),
        'ple_b_gate': nrm(ks[20], (DEPTH, D_MODEL), 0.02),
        'ln2_g': 1.0 + nrm(ks[21], (DEPTH, D_MODEL), 0.02),
        'ln2_b': nrm(ks[22], (DEPTH, D_MODEL), 0.02),
    }


def reference(x, p, w_in, b_in, gmlp_ln_g, gmlp_ln_b, gmlp_w_s, gmlp_b_s, w_proj_a, w_proj_b,
              w_out, ln1_g, ln1_b, rel_bias, peer_w_q, peer_sub_keys, peer_u, peer_v,
              ple_w_proj, ple_w_gate, ple_b_gate, ln2_g, ln2_b):
    alpha = (2.0 * DEPTH) ** 0.25
    bsz, seq, _ = x.shape
    splits = [sum(IN_WIDTHS[:j + 1]) for j in range(len(IN_WIDTHS) - 1)]
    for i in range(DEPTH):
        z = x @ w_in[i] + b_in[i]
        u_a, v_a, q_b, k_b, v_b, g_a, g_b = jnp.split(z, splits, axis=-1)
        a = gmlp_spatial_gating(jax.nn.gelu(u_a, approximate=False), jax.nn.gelu(v_a, approximate=False),
                                gmlp_ln_g[i], gmlp_ln_b[i], gmlp_w_s[i], gmlp_b_s[i])
        head_shape = (bsz, seq, MOBA_HEADS, MOBA_HEAD_DIM)
        o = moba_attention(q_b.reshape(head_shape), k_b.reshape(head_shape), v_b.reshape(head_shape), rel_bias)
        m = jax.nn.sigmoid(g_a) * (a @ w_proj_a[i]) + jax.nn.sigmoid(g_b) * (o @ w_proj_b[i])
        h = layer_norm(alpha * x + m @ w_out[i], ln1_g[i], ln1_b[i])
        f = peer_ffn(h, peer_w_q[i], peer_sub_keys[i], peer_u[i], peer_v[i])
        e = (p[i] @ ple_w_proj[i]) * jax.nn.sigmoid(h @ ple_w_gate[i] + ple_b_gate[i])
        x = layer_norm(alpha * h + f + e, ln2_g[i], ln2_b[i])
    return x
```

```python
import functools
import math

import numpy as np
import jax
import jax.numpy as jnp
from jax import lax
from jax.experimental import pallas as pl
from jax.experimental.pallas import tpu as pltpu

F32 = jnp.float32
BF16 = jnp.bfloat16
HIGHEST = lax.Precision.HIGHEST

LANES = 128
SUBLANES = 8
LN_EPS = 1e-5
NEG_INF = -1e30

GMLP_GROUPS = 8
GMLP_CHUNK = 128
MOBA_HEADS = 8
MOBA_BLOCK = 256
MOBA_TOPK = 3
REL_BUCKETS = 32
REL_MAX_DISTANCE = 1024
PEER_HEADS = 8
PEER_TOPK = 16

VMEM_LIMIT = 56 * 1024 * 1024


def _layer_norm(x, g, b):
    mu = jnp.mean(x, axis=-1, keepdims=True)
    xc = x - mu
    var = jnp.mean(xc * xc, axis=-1, keepdims=True)
    return xc * lax.rsqrt(var + LN_EPS) * g + b


def _gelu(x):
    return 0.5 * x * (1.0 + lax.erf(x * math.sqrt(0.5)))


def _split_bf16(x):
    hi = x.astype(BF16)
    lo = (x - hi.astype(F32)).astype(BF16)
    return hi, lo


def _inproj_kernel(x_ref, w_ref, b_ref, o_ref):
    seg = pl.program_id(0)
    acc = jnp.dot(x_ref[...].astype(BF16), w_ref[...], preferred_element_type=F32) + b_ref[...]

    @pl.when(seg < 2)
    def _():
        o_ref[0] = _gelu(acc)

    @pl.when(jnp.logical_and(seg >= 2, seg < 5))
    def _():
        o_ref[0] = acc

    @pl.when(seg >= 5)
    def _():
        o_ref[0] = jax.nn.sigmoid(acc)


def _in_proj(x2, w_bf, b, tm):
    n_tok, d = x2.shape
    n_seg = w_bf.shape[1] // d
    return pl.pallas_call(
        _inproj_kernel,
        grid=(n_seg, n_tok // tm),
        in_specs=[
            pl.BlockSpec((tm, d), lambda j, i: (i, 0)),
            pl.BlockSpec((d, d), lambda j, i: (0, j)),
            pl.BlockSpec((1, d), lambda j, i: (0, j)),
        ],
        out_specs=pl.BlockSpec((1, tm, d), lambda j, i: (j, i, 0)),
        out_shape=jax.ShapeDtypeStruct((n_seg, n_tok, d), F32),
        compiler_params=pltpu.CompilerParams(
            dimension_semantics=("arbitrary", "arbitrary"), vmem_limit_bytes=VMEM_LIMIT),
        name="in_proj",
    )(x2, w_bf, b)


def _moba_kernel(q_ref, k_ref, v_ref, bias_ref, o_ref, kb_scr, vb_scr, kmean_scr, *, scale):
    i = pl.program_id(2)
    seq, hd = kb_scr.shape
    blk = q_ref.shape[2]
    nb = seq // blk

    @pl.when(i == 0)
    def _():
        kmean_scr[...] = jnp.zeros_like(kmean_scr)
        for n in range(nb):
            kf = k_ref[0, 0, n * blk:(n + 1) * blk, :]
            kmean_scr[n:n + 1, :] = jnp.mean(kf, axis=0, keepdims=True)
            kb_scr[n * blk:(n + 1) * blk, :] = kf.astype(BF16)
            vb_scr[n * blk:(n + 1) * blk, :] = v_ref[0, 0, n * blk:(n + 1) * blk, :].astype(BF16)

    q = q_ref[0, 0]
    qb = q.astype(BF16)

    gate = lax.dot_general(kmean_scr[...], q, (((1,), (1,)), ((), ())),
                           precision=HIGHEST, preferred_element_type=F32)
    n_iota = lax.broadcasted_iota(jnp.int32, gate.shape, 0)
    past = n_iota < i
    g = jnp.where(past, gate, NEG_INF)
    picked = jnp.zeros(gate.shape, F32)
    for _ in range(MOBA_TOPK):
        m = jnp.max(g, axis=0, keepdims=True)
        first = jnp.min(jnp.where(g == m, n_iota, LANES), axis=0, keepdims=True)
        hit = n_iota == first
        picked = jnp.where(hit, 1.0, picked)
        g = jnp.where(hit, -jnp.inf, g)
    sel_neg = jnp.where(jnp.logical_and(picked > 0.5, past), 0.0, NEG_INF / scale)
    q_aug = jnp.concatenate([qb, sel_neg.T.astype(BF16)], axis=1)

    row = lax.broadcasted_iota(jnp.int32, (blk, blk), 0)
    col = lax.broadcasted_iota(jnp.int32, (blk, blk), 1)
    own = pl.multiple_of(i * blk, blk)
    s = lax.dot_general(qb, kb_scr[pl.ds(own, blk), :], (((1,), (1,)), ((), ())),
                        preferred_element_type=F32) * scale + bias_ref[0, 0]
    s = jnp.where(row >= col, s, NEG_INF)
    m0 = jnp.max(s, axis=1, keepdims=True)
    p = jnp.exp(s - m0)
    l0 = jnp.sum(p, axis=1, keepdims=True)
    acc0 = jnp.dot(p.astype(BF16), vb_scr[pl.ds(own, blk), :], preferred_element_type=F32)

    lane_blk = lax.broadcasted_iota(jnp.int32, (blk, LANES), 1)

    def past_block(j, carry):
        m_run, l_run, acc = carry
        start = pl.multiple_of(j * blk, blk)
        onehot = jnp.where(lane_blk == j, 1.0, 0.0).astype(BF16)
        k_aug = jnp.concatenate([kb_scr[pl.ds(start, blk), :], onehot], axis=1)
        sj = lax.dot_general(q_aug, k_aug, (((1,), (1,)), ((), ())),
                             preferred_element_type=F32) * scale + bias_ref[0, i - j]
        m_new = jnp.maximum(m_run, jnp.max(sj, axis=1, keepdims=True))
        corr = jnp.exp(m_run - m_new)
        pj = jnp.exp(sj - m_new)
        l_new = corr * l_run + jnp.sum(pj, axis=1, keepdims=True)
        acc_new = corr * acc + jnp.dot(pj.astype(BF16), vb_scr[pl.ds(start, blk), :],
                                       preferred_element_type=F32)
        return m_new, l_new, acc_new

    _, l_fin, acc_fin = lax.fori_loop(0, i, past_block, (m0, l0, acc0))
    o_ref[0] = acc_fin / l_fin


def _t5_bucket(dist):
    n = jnp.maximum(dist, 0)
    max_exact = REL_BUCKETS // 2
    nf = jnp.maximum(n, max_exact).astype(jnp.float32)
    large = max_exact + (jnp.log(nf / max_exact) / math.log(REL_MAX_DISTANCE / max_exact)
                         * (REL_BUCKETS - max_exact)).astype(jnp.int32)
    large = jnp.minimum(large, REL_BUCKETS - 1)
    return jnp.where(n < max_exact, n, large)


def _bias_tiles(rel_bias, nb, blk):
    delta = np.arange(nb)[:, None, None]
    tq = np.arange(blk)[None, :, None]
    ts = np.arange(blk)[None, None, :]
    dist = jnp.asarray(delta * blk + tq - ts, dtype=jnp.int32)
    return rel_bias.T.astype(F32)[:, _t5_bucket(dist)]


def _moba(z4, bias, n_heads):
    _, bsz, seq, d = z4.shape
    hd = d // n_heads
    blk = MOBA_BLOCK
    nb = seq // blk
    assert nb <= LANES and seq % blk == 0
    kern = functools.partial(_moba_kernel, scale=hd ** -0.5)
    return pl.pallas_call(
        kern,
        grid=(bsz, n_heads, nb),
        in_specs=[
            pl.BlockSpec((1, 1, blk, hd), lambda b, h, i: (2, b, i, h)),
            pl.BlockSpec((1, 1, seq, hd), lambda b, h, i: (3, b, 0, h)),
            pl.BlockSpec((1, 1, seq, hd), lambda b, h, i: (4, b, 0, h)),
            pl.BlockSpec((1, nb, blk, blk), lambda b, h, i: (h, 0, 0, 0)),
        ],
        out_specs=pl.BlockSpec((1, blk, hd), lambda b, h, i: (b, i, h)),
        out_shape=jax.ShapeDtypeStruct((bsz, seq, d), F32),
        scratch_shapes=[
            pltpu.VMEM((seq, hd), BF16),
            pltpu.VMEM((seq, hd), BF16),
            pltpu.VMEM((LANES, hd), F32),
        ],
        compiler_params=pltpu.CompilerParams(
            dimension_semantics=("arbitrary", "arbitrary", "arbitrary"),
            vmem_limit_bytes=VMEM_LIMIT),
        name="moba",
    )(z4, z4, z4, bias)


def _mix_kernel(ug_ref, vg_ref, sa_ref, sb_ref, o_ref, x_ref, lng_ref, lnb_ref, ws_ref, bs_ref,
                pa_ref, pb_ref, wo_ref, g1_ref, b1_ref, h_ref, *, alpha):
    tm, d = x_ref.shape
    gw = d // GMLP_GROUPS
    vb = _layer_norm(vg_ref[0], lng_ref[...], lnb_ref[...]).astype(BF16)
    rows = []
    for n in range(tm // GMLP_CHUNK):
        cols = []
        for g in range(GMLP_GROUPS):
            vc = vb[n * GMLP_CHUNK:(n + 1) * GMLP_CHUNK, g * gw:(g + 1) * gw]
            cols.append(jnp.dot(ws_ref[g], vc, preferred_element_type=F32) + bs_ref[g])
        rows.append(jnp.concatenate(cols, axis=1))
    a = ug_ref[0] * jnp.concatenate(rows, axis=0)
    ma = jnp.dot(a.astype(BF16), pa_ref[...], preferred_element_type=F32)
    mb = jnp.dot(o_ref[...].astype(BF16), pb_ref[...], preferred_element_type=F32)
    m = sa_ref[0] * ma + sb_ref[0] * mb
    y = alpha * x_ref[...] + jnp.dot(m.astype(BF16), wo_ref[...], preferred_element_type=F32)
    h_ref[...] = _layer_norm(y, g1_ref[...], b1_ref[...])


def _mix(z, o2, x2, lng, lnb, ws_bf, bs_b, pa_bf, pb_bf, wo_bf, g1, b1, alpha, tm):
    n_tok, d = x2.shape
    full = lambda shape: pl.BlockSpec(shape, lambda i: (0,) * len(shape))
    seg = lambda s: pl.BlockSpec((1, tm, d), lambda i, s=s: (s, i, 0))
    tok = pl.BlockSpec((tm, d), lambda i: (i, 0))
    return pl.pallas_call(
        functools.partial(_mix_kernel, alpha=alpha),
        grid=(n_tok // tm,),
        in_specs=[seg(0), seg(1), seg(5), seg(6), tok, tok,
                  full((1, d)), full((1, d)), full(ws_bf.shape), full(bs_b.shape),
                  full((d, d)), full((d, d)), full((d, d)), full((1, d)), full((1, d))],
        out_specs=tok,
        out_shape=jax.ShapeDtypeStruct((n_tok, d), F32),
        compiler_params=pltpu.CompilerParams(
            dimension_semantics=("arbitrary",), vmem_limit_bytes=VMEM_LIMIT),
        name="mix",
    )(z, z, z, z, o2, x2, lng, lnb, ws_bf, bs_b, pa_bf, pb_bf, wo_bf, g1, b1)


def _staircase_pieces(topk):
    pieces = [("col", 0, topk)]
    for b in range(1, SUBLANES):
        pieces.append(("col", b, SUBLANES))
    pieces.append(("row", SUBLANES, topk - SUBLANES))
    return pieces


def _peer_select_kernel(h_ref, wq_ref, sk_ref, exp_ref, gate_ref, ts_scr, ti_scr, bs_scr, be_scr,
                        *, n_heads, topk):
    tm = h_ref.shape[0]
    nk, kh = sk_ref.shape[1], sk_ref.shape[2]
    q = jnp.dot(h_ref[...].astype(BF16), wq_ref[...], preferred_element_type=F32)
    qb = q.astype(BF16)
    key_iota = lax.broadcasted_iota(jnp.int32, (nk, tm), 0)

    pieces = _staircase_pieces(topk)
    flat_parts, valid_parts = [], []
    for kind, b, rows in pieces:
        r_iota = lax.broadcasted_iota(jnp.int32, (rows, tm), 0)
        if kind == "col":
            flat_parts.append(r_iota * topk + b)
            valid_parts.append(r_iota < topk // (b + 1))
        else:
            flat_parts.append(r_iota + b)
            valid_parts.append(r_iota >= 0)
    flat = jnp.concatenate(flat_parts, axis=0)
    valid = jnp.concatenate(valid_parts, axis=0)
    big = topk * topk

    for hh in range(n_heads):
        for half in range(2):
            c0 = (hh * 2 + half) * kh
            s = lax.dot_general(sk_ref[half], qb[:, c0:c0 + kh], (((1,), (1,)), ((), ())),
                                preferred_element_type=F32)
            for r in range(topk):
                m = jnp.max(s, axis=0, keepdims=True)
                idx = jnp.min(jnp.where(s == m, key_iota, nk), axis=0, keepdims=True)
                ts_scr[half, r:r + 1, :] = m
                ti_scr[half, r:r + 1, :] = idx
                s = jnp.where(key_iota == idx, -jnp.inf, s)
        t0, t1 = ts_scr[0], ts_scr[1]
        i0, i1 = ti_scr[0], ti_scr[1]
        cs, ce = [], []
        for kind, b, rows in pieces:
            if kind == "col":
                cs.append(t0[0:rows] + t1[b:b + 1])
                ce.append(i0[0:rows] * nk + i1[b:b + 1])
            else:
                cs.append(t0[0:1] + t1[b:b + rows])
                ce.append(i0[0:1] * nk + i1[b:b + rows])
        cand = jnp.where(valid, jnp.concatenate(cs, axis=0), -jnp.inf)
        cexp = jnp.concatenate(ce, axis=0)
        for r in range(topk):
            m = jnp.max(cand, axis=0, keepdims=True)
            fsel = jnp.min(jnp.where(cand == m, flat, big), axis=0, keepdims=True)
            hit = flat == fsel
            bs_scr[r:r + 1, :] = m
            be_scr[r:r + 1, :] = jnp.max(jnp.where(hit, cexp, -1), axis=0, keepdims=True)
            cand = jnp.where(hit, -jnp.inf, cand)
        best = bs_scr[...]
        ex = jnp.exp(best - jnp.max(best, axis=0, keepdims=True))
        gate_ref[hh * topk:(hh + 1) * topk, :] = ex / jnp.sum(ex, axis=0, keepdims=True)
        exp_ref[hh * topk:(hh + 1) * topk, :] = be_scr[...]


def _peer_select(h2, wq_bf, sk_bf, tm):
    n_tok, d = h2.shape
    n_slots = PEER_HEADS * PEER_TOPK
    full = lambda shape: pl.BlockSpec(shape, lambda i: (0,) * len(shape))
    kern = functools.partial(_peer_select_kernel, n_heads=PEER_HEADS, topk=PEER_TOPK)
    return pl.pallas_call(
        kern,
        grid=(n_tok // tm,),
        in_specs=[pl.BlockSpec((tm, d), lambda i: (i, 0)), full(wq_bf.shape), full(sk_bf.shape)],
        out_specs=[pl.BlockSpec((n_slots, tm), lambda i: (0, i)),
                   pl.BlockSpec((n_slots, tm), lambda i: (0, i))],
        out_shape=[jax.ShapeDtypeStruct((n_slots, n_tok), jnp.int32),
                   jax.ShapeDtypeStruct((n_slots, n_tok), F32)],
        scratch_shapes=[pltpu.VMEM((2, PEER_TOPK, tm), F32), pltpu.VMEM((2, PEER_TOPK, tm), jnp.int32),
                        pltpu.VMEM((PEER_TOPK, tm), F32), pltpu.VMEM((PEER_TOPK, tm), jnp.int32)],
        compiler_params=pltpu.CompilerParams(
            dimension_semantics=("arbitrary",), vmem_limit_bytes=VMEM_LIMIT),
        name="peer_select",
    )(h2, wq_bf, sk_bf)


ROWS_PER_EXPERT = 4


def _pack_table(tab):
    n_exp, d = tab.shape
    assert d == 2 * ROWS_PER_EXPERT * LANES
    bits = lax.bitcast_convert_type(tab.astype(BF16), jnp.uint16).astype(jnp.uint32)
    bits = bits.reshape(n_exp, ROWS_PER_EXPERT, 2, LANES)
    words = bits[:, :, 0, :] | (bits[:, :, 1, :] << 16)
    return words.reshape(n_exp * ROWS_PER_EXPERT, LANES)


def _load_table(tbl_hbm, tbl_vmem, sem):
    @pl.when(pl.program_id(0) == 0)
    def _():
        cp = pltpu.make_async_copy(tbl_hbm, tbl_vmem, sem)
        cp.start()
        cp.wait()


def _gather_rows(idx_ref, t, tbl_vmem, tile_ref, n_slots):
    for k in range(n_slots):
        row = pl.multiple_of(idx_ref[t, k] * ROWS_PER_EXPERT, ROWS_PER_EXPERT)
        tile_ref[k * ROWS_PER_EXPERT:(k + 1) * ROWS_PER_EXPERT, :] = tbl_vmem[pl.ds(row, ROWS_PER_EXPERT), :]


def _chunk_mask(width):
    r = lax.broadcasted_iota(jnp.int32, (SUBLANES, width), 0)
    j = lax.broadcasted_iota(jnp.int32, (SUBLANES, width), 1)
    return jnp.where(j % SUBLANES == r, 1.0, 0.0).astype(F32)


def _peer_u_kernel(idx_ref, x_ref, tbl_hbm, pair_ref, hid_ref, tbl_vmem, tile_ref, hsc_ref, sem):
    _load_table(tbl_hbm, tbl_vmem, sem)
    tt = x_ref.shape[0]
    n_slots = idx_ref.shape[1]
    mask = _chunk_mask(n_slots * SUBLANES)

    def token(t, carry):
        _gather_rows(idx_ref, t, tbl_vmem, tile_ref, n_slots)
        hi, lo = _split_bf16(x_ref[t])
        lhs = jnp.concatenate([hi, lo], axis=0)
        rhs = pltpu.bitcast(tile_ref[...], BF16)
        out = lax.dot_general(lhs, rhs, (((1,), (1,)), ((), ())), preferred_element_type=F32)
        z = (out[0:SUBLANES] + out[SUBLANES:]) * mask
        hsc_ref[pl.ds(t, 1), :] = jnp.sum(z, axis=0, keepdims=True)
        return carry

    lax.fori_loop(0, tt, token, 0)
    hid_ref[...] = jnp.dot(hsc_ref[...], pair_ref[...], precision=HIGHEST, preferred_element_type=F32)


def _peer_v_kernel(idx_ref, hid_ref, gate_ref, tbl_hbm, rep_ref, y_ref, tbl_vmem, tile_ref, wrep_ref, sem):
    _load_table(tbl_hbm, tbl_vmem, sem)
    tt = y_ref.shape[0]
    n_slots = idx_ref.shape[1]
    mask = _chunk_mask(n_slots * SUBLANES)
    w = gate_ref[...] * _gelu(hid_ref[...])
    wrep_ref[...] = jnp.dot(w, rep_ref[...], precision=HIGHEST, preferred_element_type=F32)

    def token(t, carry):
        _gather_rows(idx_ref, t, tbl_vmem, tile_ref, n_slots)
        hi, lo = _split_bf16(wrep_ref[pl.ds(t, 1), :] * mask)
        lhs = jnp.concatenate([hi, lo], axis=0)
        rhs = pltpu.bitcast(tile_ref[...], BF16)
        out = jnp.dot(lhs, rhs, preferred_element_type=F32)
        y_ref[t] = out[0:SUBLANES] + out[SUBLANES:]
        return carry

    lax.fori_loop(0, tt, token, 0)


def _peer_scratch(tbl, tt, n_slots):
    return [pltpu.VMEM(tbl.shape, jnp.uint32),
            pltpu.VMEM((n_slots * ROWS_PER_EXPERT, LANES), jnp.uint32),
            pltpu.VMEM((tt, n_slots * SUBLANES), F32),
            pltpu.SemaphoreType.DMA(())]


def _peer_u(idx, h3, tbl, tt):
    n_tok, n_slots = idx.shape
    width = n_slots * SUBLANES
    pair = (np.arange(width)[:, None] // SUBLANES == np.arange(n_slots)[None, :]).astype(np.float32)
    return pl.pallas_call(
        _peer_u_kernel,
        grid=(n_tok // tt,),
        in_specs=[pl.BlockSpec((tt, n_slots), lambda i: (i, 0), memory_space=pltpu.SMEM),
                  pl.BlockSpec((tt, SUBLANES, LANES), lambda i: (i, 0, 0)),
                  pl.BlockSpec(memory_space=pl.ANY),
                  pl.BlockSpec((width, n_slots), lambda i: (0, 0))],
        out_specs=pl.BlockSpec((tt, n_slots), lambda i: (i, 0)),
        out_shape=jax.ShapeDtypeStruct((n_tok, n_slots), F32),
        scratch_shapes=_peer_scratch(tbl, tt, n_slots),
        compiler_params=pltpu.CompilerParams(
            dimension_semantics=("arbitrary",), vmem_limit_bytes=VMEM_LIMIT),
        name="peer_u",
    )(idx, h3, tbl, jnp.asarray(pair))


def _peer_v(idx, hid, gates, tbl, tt):
    n_tok, n_slots = idx.shape
    width = n_slots * SUBLANES
    rep = (np.arange(n_slots)[:, None] == np.arange(width)[None, :] // SUBLANES).astype(np.float32)
    tok = pl.BlockSpec((tt, n_slots), lambda i: (i, 0))
    return pl.pallas_call(
        _peer_v_kernel,
        grid=(n_tok // tt,),
        in_specs=[pl.BlockSpec((tt, n_slots), lambda i: (i, 0), memory_space=pltpu.SMEM),
                  tok, tok,
                  pl.BlockSpec(memory_space=pl.ANY),
                  pl.BlockSpec((n_slots, width), lambda i: (0, 0))],
        out_specs=pl.BlockSpec((tt, SUBLANES, LANES), lambda i: (i, 0, 0)),
        out_shape=jax.ShapeDtypeStruct((n_tok, SUBLANES, LANES), F32),
        scratch_shapes=_peer_scratch(tbl, tt, n_slots),
        compiler_params=pltpu.CompilerParams(
            dimension_semantics=("arbitrary",), vmem_limit_bytes=VMEM_LIMIT),
        name="peer_v",
    )(idx, hid, gates, tbl, jnp.asarray(rep))


def _final_kernel(h_ref, f_ref, p_ref, wple_ref, wpg_ref, bpg_ref, g_ref, b_ref, o_ref, *, alpha):
    h = h_ref[...]
    gate = jax.nn.sigmoid(jnp.dot(h.astype(BF16), wpg_ref[...], preferred_element_type=F32) + bpg_ref[...])
    e = jnp.dot(p_ref[...].astype(BF16), wple_ref[...], preferred_element_type=F32) * gate
    o_ref[...] = _layer_norm(alpha * h + f_ref[...] + e, g_ref[...], b_ref[...])


def _final(h2, f2, p2, wple_bf, wpg_bf, bpg, g2, b2, alpha, tm):
    n_tok, d = h2.shape
    pd = p2.shape[1]
    full = lambda shape: pl.BlockSpec(shape, lambda i: (0,) * len(shape))
    tok = pl.BlockSpec((tm, d), lambda i: (i, 0))
    return pl.pallas_call(
        functools.partial(_final_kernel, alpha=alpha),
        grid=(n_tok // tm,),
        in_specs=[tok, tok, pl.BlockSpec((tm, pd), lambda i: (i, 0)),
                  full((pd, d)), full((d, d)), full((1, d)), full((1, d)), full((1, d))],
        out_specs=tok,
        out_shape=jax.ShapeDtypeStruct((n_tok, d), F32),
        compiler_params=pltpu.CompilerParams(
            dimension_semantics=("arbitrary",), vmem_limit_bytes=VMEM_LIMIT),
        name="final",
    )(h2, f2, p2, wple_bf, wpg_bf, bpg, g2, b2)


def _token_tile(n_tok, want):
    tm = min(want, n_tok)
    assert n_tok % tm == 0
    return tm


def kernel(x, p, w_in, b_in, gmlp_ln_g, gmlp_ln_b, gmlp_w_s, gmlp_b_s, w_proj_a, w_proj_b, w_out,
           ln1_g, ln1_b, rel_bias, peer_w_q, peer_sub_keys, peer_u, peer_v, ple_w_proj, ple_w_gate,
           ple_b_gate, ln2_g, ln2_b):
    depth = w_in.shape[0]
    bsz, seq, d = x.shape
    n_tok = bsz * seq
    alpha = (2.0 * depth) ** 0.25
    assert d == GMLP_GROUPS * GMLP_CHUNK and seq % MOBA_BLOCK == 0
    row = lambda v: v.reshape(1, -1)
    bias = _bias_tiles(rel_bias, seq // MOBA_BLOCK, MOBA_BLOCK)
    x2 = x.reshape(n_tok, d)
    for i in range(depth):
        z = _in_proj(x2, w_in[i].astype(BF16), row(b_in[i]), _token_tile(n_tok, 512))
        o = _moba(z.reshape(z.shape[0], bsz, seq, d), bias, MOBA_HEADS)
        bs_b = jnp.broadcast_to(gmlp_b_s[i][:, :, None], gmlp_b_s[i].shape + (d // GMLP_GROUPS,))
        h2 = _mix(z, o.reshape(n_tok, d), x2, row(gmlp_ln_g[i]), row(gmlp_ln_b[i]),
                  jnp.tril(gmlp_w_s[i]).astype(BF16), bs_b,
                  w_proj_a[i].astype(BF16), w_proj_b[i].astype(BF16), w_out[i].astype(BF16),
                  row(ln1_g[i]), row(ln1_b[i]), alpha, _token_tile(n_tok, 256))
        experts_t, gates_t = _peer_select(h2, peer_w_q[i].astype(BF16), peer_sub_keys[i].astype(BF16),
                                          _token_tile(n_tok, 256))
        experts, gates = experts_t.T, gates_t.T
        tt = _token_tile(n_tok, 64)
        hid = _peer_u(experts, h2.reshape(n_tok, SUBLANES, LANES), _pack_table(peer_u[i]), tt)
        f = _peer_v(experts, hid, gates, _pack_table(peer_v[i]), tt)
        x2 = _final(h2, f.reshape(n_tok, d), p[i].reshape(n_tok, -1), ple_w_proj[i].astype(BF16),
                    ple_w_gate[i].astype(BF16), row(ple_b_gate[i]), row(ln2_g[i]), row(ln2_b[i]),
                    alpha, _token_tile(n_tok, 512))
    return x2.reshape(bsz, seq, d)
```

```python
import functools
import math

import numpy as np
import jax
import jax.numpy as jnp
from jax import lax
from jax.experimental import pallas as pl
from jax.experimental.pallas import tpu as pltpu

F32 = jnp.float32
BF16 = jnp.bfloat16
HIGHEST = lax.Precision.HIGHEST

LANES = 128
SUBLANES = 8
LN_EPS = 1e-5
NEG_INF = -1e30

GMLP_GROUPS = 8
GMLP_CHUNK = 128
MOBA_HEADS = 8
MOBA_BLOCK = 256
MOBA_TOPK = 3
REL_BUCKETS = 32
REL_MAX_DISTANCE = 1024
PEER_HEADS = 8
PEER_TOPK = 16

VMEM_LIMIT = 56 * 1024 * 1024


def _layer_norm(x, g, b):
    mu = jnp.mean(x, axis=-1, keepdims=True)
    xc = x - mu
    var = jnp.mean(xc * xc, axis=-1, keepdims=True)
    return xc * lax.rsqrt(var + LN_EPS) * g + b


def _gelu(x):
    return 0.5 * x * (1.0 + lax.erf(x * math.sqrt(0.5)))


def _split_bf16(x):
    hi = x.astype(BF16)
    lo = (x - hi.astype(F32)).astype(BF16)
    return hi, lo


def _inproj_kernel(x_ref, w_ref, b_ref, o_ref):
    seg = pl.program_id(0)
    acc = jnp.dot(x_ref[...].astype(BF16), w_ref[...], preferred_element_type=F32) + b_ref[...]

    @pl.when(seg < 2)
    def _():
        o_ref[0] = _gelu(acc)

    @pl.when(jnp.logical_and(seg >= 2, seg < 5))
    def _():
        o_ref[0] = acc

    @pl.when(seg >= 5)
    def _():
        o_ref[0] = jax.nn.sigmoid(acc)


def _in_proj(x2, w_bf, b, tm):
    n_tok, d = x2.shape
    n_seg = w_bf.shape[1] // d
    return pl.pallas_call(
        _inproj_kernel,
        grid=(n_seg, n_tok // tm),
        in_specs=[
            pl.BlockSpec((tm, d), lambda j, i: (i, 0)),
            pl.BlockSpec((d, d), lambda j, i: (0, j)),
            pl.BlockSpec((1, d), lambda j, i: (0, j)),
        ],
        out_specs=pl.BlockSpec((1, tm, d), lambda j, i: (j, i, 0)),
        out_shape=jax.ShapeDtypeStruct((n_seg, n_tok, d), F32),
        compiler_params=pltpu.CompilerParams(
            dimension_semantics=("arbitrary", "arbitrary"), vmem_limit_bytes=VMEM_LIMIT),
        name="in_proj",
    )(x2, w_bf, b)


def _moba_kernel(q_ref, k_ref, v_ref, bias_ref, o_ref, kb_scr, vb_scr, kmean_scr, *, scale):
    i = pl.program_id(2)
    seq, hd = kb_scr.shape
    blk = q_ref.shape[2]
    nb = seq // blk
    nbp = kmean_scr.shape[0]
    pair = 2 * blk

    @pl.when(i == 0)
    def _():
        kmean_scr[...] = jnp.zeros_like(kmean_scr)
        for n in range(nb):
            kf = k_ref[0, 0, n * blk:(n + 1) * blk, :]
            kmean_scr[n:n + 1, :] = jnp.mean(kf, axis=0, keepdims=True)
            kb_scr[n * blk:(n + 1) * blk, :] = kf.astype(BF16)
            vb_scr[n * blk:(n + 1) * blk, :] = v_ref[0, 0, n * blk:(n + 1) * blk, :].astype(BF16)

    q = q_ref[0, 0]

    gate = lax.dot_general(kmean_scr[...], q, (((1,), (1,)), ((), ())),
                           precision=HIGHEST, preferred_element_type=F32)
    n_iota = lax.broadcasted_iota(jnp.int32, gate.shape, 0)
    past = n_iota < i
    g = jnp.where(past, gate, NEG_INF)
    picked = jnp.zeros(gate.shape, F32)
    for _ in range(MOBA_TOPK):
        m = jnp.max(g, axis=0, keepdims=True)
        first = jnp.min(jnp.where(g == m, n_iota, nbp), axis=0, keepdims=True)
        hit = n_iota == first
        picked = jnp.where(hit, 1.0, picked)
        g = jnp.where(hit, -jnp.inf, g)
    allowed = jnp.logical_or(jnp.logical_and(picked > 0.5, past), n_iota == i)
    sel_neg = jnp.where(allowed, 0.0, NEG_INF)
    sel_pad = jnp.concatenate([sel_neg, jnp.full((LANES - nbp, blk), NEG_INF, F32)], axis=0)
    q_aug = jnp.concatenate([(q * scale).astype(BF16), sel_pad.T.astype(BF16)], axis=1)

    row = lax.broadcasted_iota(jnp.int32, (blk, blk), 0)
    col = lax.broadcasted_iota(jnp.int32, (blk, blk), 1)
    tri = row >= col
    lane_blk = lax.broadcasted_iota(jnp.int32, (pair, LANES), 1)
    second = (lax.broadcasted_iota(jnp.int32, (pair, LANES), 0) >= blk).astype(jnp.int32)
    n_pairs = (i + 2) // 2

    def block_pair(it, carry):
        m_run, l_run, acc = carry
        n0 = 2 * (n_pairs - 1 - it)
        start = pl.multiple_of(n0 * blk, pair)
        onehot = jnp.where(lane_blk == n0 + second, 1.0, 0.0).astype(BF16)
        k_aug = jnp.concatenate([kb_scr[pl.ds(start, pair), :], onehot], axis=1)
        s = lax.dot_general(q_aug, k_aug, (((1,), (1,)), ((), ())), preferred_element_type=F32)
        s0 = s[:, :blk] + bias_ref[0, i - n0]
        s1 = s[:, blk:] + bias_ref[0, jnp.maximum(i - n0 - 1, 0)]
        s0 = jnp.where(jnp.logical_or(tri, n0 < i), s0, NEG_INF)
        s1 = jnp.where(jnp.logical_or(tri, n0 + 1 < i), s1, NEG_INF)
        s = jnp.concatenate([s0, s1], axis=1)
        m_new = jnp.maximum(m_run, jnp.max(s, axis=1, keepdims=True))
        corr = jnp.exp(m_run - m_new)
        p = jnp.exp(s - m_new)
        l_new = corr * l_run + jnp.sum(p, axis=1, keepdims=True)
        acc_new = corr * acc + jnp.dot(p.astype(BF16), vb_scr[pl.ds(start, pair), :],
                                       preferred_element_type=F32)
        return m_new, l_new, acc_new

    init = (jnp.full((blk, 1), NEG_INF, F32), jnp.zeros((blk, 1), F32), jnp.zeros((blk, hd), F32))
    _, l_fin, acc_fin = lax.fori_loop(0, n_pairs, block_pair, init)
    o_ref[0] = acc_fin / l_fin


def _t5_bucket(dist):
    n = jnp.maximum(dist, 0)
    max_exact = REL_BUCKETS // 2
    nf = jnp.maximum(n, max_exact).astype(jnp.float32)
    large = max_exact + (jnp.log(nf / max_exact) / math.log(REL_MAX_DISTANCE / max_exact)
                         * (REL_BUCKETS - max_exact)).astype(jnp.int32)
    large = jnp.minimum(large, REL_BUCKETS - 1)
    return jnp.where(n < max_exact, n, large)


def _bias_tiles(rel_bias, nb, blk):
    n_heads = rel_bias.shape[1]
    dist = jnp.arange(-(blk - 1), nb * blk, dtype=jnp.int32)
    bvec = rel_bias.T.astype(F32)[:, _t5_bucket(dist)]
    period = 2 * blk - 1
    win = jnp.stack([bvec[:, d * blk:d * blk + period] for d in range(nb)], axis=1)
    rev = win[..., ::-1]
    tiled = jnp.tile(rev, (1, 1, blk + 1))[..., :2 * blk * blk]
    rows = tiled.reshape(n_heads, nb, blk, 2 * blk)[..., :blk]
    return rows[:, :, ::-1, :]


def _moba(z4, bias, n_heads):
    _, bsz, seq, d = z4.shape
    hd = d // n_heads
    blk = MOBA_BLOCK
    nb = seq // blk
    assert nb <= LANES and nb % 2 == 0 and seq % blk == 0
    kern = functools.partial(_moba_kernel, scale=hd ** -0.5)
    return pl.pallas_call(
        kern,
        grid=(bsz, n_heads, nb),
        in_specs=[
            pl.BlockSpec((1, 1, blk, hd), lambda b, h, i: (2, b, i, h)),
            pl.BlockSpec((1, 1, seq, hd), lambda b, h, i: (3, b, 0, h)),
            pl.BlockSpec((1, 1, seq, hd), lambda b, h, i: (4, b, 0, h)),
            pl.BlockSpec((1, nb, blk, blk), lambda b, h, i: (h, 0, 0, 0)),
        ],
        out_specs=pl.BlockSpec((1, blk, hd), lambda b, h, i: (b, i, h)),
        out_shape=jax.ShapeDtypeStruct((bsz, seq, d), F32),
        scratch_shapes=[
            pltpu.VMEM((seq, hd), BF16),
            pltpu.VMEM((seq, hd), BF16),
            pltpu.VMEM((-(-nb // SUBLANES) * SUBLANES, hd), F32),
        ],
        compiler_params=pltpu.CompilerParams(
            dimension_semantics=("arbitrary", "arbitrary", "arbitrary"),
            vmem_limit_bytes=VMEM_LIMIT),
        name="moba",
    )(z4, z4, z4, bias)


def _mix_kernel(ug_ref, vg_ref, sa_ref, sb_ref, o_ref, x_ref, lng_ref, lnb_ref, ws_ref, bs_ref,
                pa_ref, pb_ref, wo_ref, g1_ref, b1_ref, h_ref, *, alpha):
    tm, d = x_ref.shape
    gw = d // GMLP_GROUPS
    vb = _layer_norm(vg_ref[0], lng_ref[...], lnb_ref[...]).astype(BF16)
    rows = []
    for n in range(tm // GMLP_CHUNK):
        cols = []
        for g in range(GMLP_GROUPS):
            vc = vb[n * GMLP_CHUNK:(n + 1) * GMLP_CHUNK, g * gw:(g + 1) * gw]
            cols.append(jnp.dot(ws_ref[g], vc, preferred_element_type=F32) + bs_ref[g])
        rows.append(jnp.concatenate(cols, axis=1))
    a = ug_ref[0] * jnp.concatenate(rows, axis=0)
    ma = jnp.dot(a.astype(BF16), pa_ref[...], preferred_element_type=F32)
    mb = jnp.dot(o_ref[...].astype(BF16), pb_ref[...], preferred_element_type=F32)
    m = sa_ref[0] * ma + sb_ref[0] * mb
    y = alpha * x_ref[...] + jnp.dot(m.astype(BF16), wo_ref[...], preferred_element_type=F32)
    h_ref[...] = _layer_norm(y, g1_ref[...], b1_ref[...])


def _mix(z, o2, x2, lng, lnb, ws_bf, bs_b, pa_bf, pb_bf, wo_bf, g1, b1, alpha, tm):
    n_tok, d = x2.shape
    full = lambda shape: pl.BlockSpec(shape, lambda i: (0,) * len(shape))
    seg = lambda s: pl.BlockSpec((1, tm, d), lambda i, s=s: (s, i, 0))
    tok = pl.BlockSpec((tm, d), lambda i: (i, 0))
    return pl.pallas_call(
        functools.partial(_mix_kernel, alpha=alpha),
        grid=(n_tok // tm,),
        in_specs=[seg(0), seg(1), seg(5), seg(6), tok, tok,
                  full((1, d)), full((1, d)), full(ws_bf.shape), full(bs_b.shape),
                  full((d, d)), full((d, d)), full((d, d)), full((1, d)), full((1, d))],
        out_specs=tok,
        out_shape=jax.ShapeDtypeStruct((n_tok, d), F32),
        compiler_params=pltpu.CompilerParams(
            dimension_semantics=("arbitrary",), vmem_limit_bytes=VMEM_LIMIT),
        name="mix",
    )(z, z, z, z, o2, x2, lng, lnb, ws_bf, bs_b, pa_bf, pb_bf, wo_bf, g1, b1)


def _staircase_pieces(topk):
    pieces = [("col", 0, topk)]
    for b in range(1, SUBLANES):
        pieces.append(("col", b, SUBLANES))
    pieces.append(("row", SUBLANES, topk - SUBLANES))
    return pieces


def _peer_select_kernel(h_ref, wq_ref, sk_ref, exp_ref, gate_ref, ts_scr, ti_scr, bs_scr, be_scr,
                        *, n_heads, topk, row_scale):
    tm = h_ref.shape[0]
    nk, kh = sk_ref.shape[1], sk_ref.shape[2]
    q = jnp.dot(h_ref[...].astype(BF16), wq_ref[...], preferred_element_type=F32)
    qb = q.astype(BF16)
    key_iota = lax.broadcasted_iota(jnp.int32, (nk, tm), 0).astype(F32)

    pieces = _staircase_pieces(topk)
    flat_parts, valid_parts = [], []
    for kind, b, rows in pieces:
        r_iota = lax.broadcasted_iota(jnp.int32, (rows, tm), 0)
        if kind == "col":
            flat_parts.append(r_iota * topk + b)
            valid_parts.append(r_iota < topk // (b + 1))
        else:
            flat_parts.append(r_iota + b)
            valid_parts.append(r_iota >= 0)
    flat = jnp.concatenate(flat_parts, axis=0).astype(F32)
    valid = jnp.concatenate(valid_parts, axis=0)
    big = float(topk * topk)

    for hh in range(n_heads):
        for half in range(2):
            c0 = (hh * 2 + half) * kh
            s = lax.dot_general(sk_ref[half], qb[:, c0:c0 + kh], (((1,), (1,)), ((), ())),
                                preferred_element_type=F32)
            for r in range(topk):
                m = jnp.max(s, axis=0, keepdims=True)
                idx = jnp.min(jnp.where(s == m, key_iota, float(nk)), axis=0, keepdims=True)
                ts_scr[half, r:r + 1, :] = m
                ti_scr[half, r:r + 1, :] = idx
                s = jnp.where(key_iota == idx, -jnp.inf, s)
        t0, t1 = ts_scr[0], ts_scr[1]
        i0, i1 = ti_scr[0], ti_scr[1]
        cs, ce = [], []
        for kind, b, rows in pieces:
            if kind == "col":
                cs.append(t0[0:rows] + t1[b:b + 1])
                ce.append(i0[0:rows] * float(nk) + i1[b:b + 1])
            else:
                cs.append(t0[0:1] + t1[b:b + rows])
                ce.append(i0[0:1] * float(nk) + i1[b:b + rows])
        cand = jnp.where(valid, jnp.concatenate(cs, axis=0), -jnp.inf)
        cexp = jnp.concatenate(ce, axis=0)
        for r in range(topk):
            m = jnp.max(cand, axis=0, keepdims=True)
            fsel = jnp.min(jnp.where(cand == m, flat, big), axis=0, keepdims=True)
            hit = flat == fsel
            bs_scr[r:r + 1, :] = m
            be_scr[r:r + 1, :] = jnp.max(jnp.where(hit, cexp, -1.0), axis=0, keepdims=True)
            cand = jnp.where(hit, -jnp.inf, cand)
        best = bs_scr[...]
        ex = jnp.exp(best - jnp.max(best, axis=0, keepdims=True))
        gate_ref[hh * topk:(hh + 1) * topk, :] = ex / jnp.sum(ex, axis=0, keepdims=True)
        exp_ref[hh * topk:(hh + 1) * topk, :] = (be_scr[...] * float(row_scale)).astype(jnp.int32)


def _peer_select(h2, wq_bf, sk_bf, tm):
    n_tok, d = h2.shape
    n_slots = PEER_HEADS * PEER_TOPK
    full = lambda shape: pl.BlockSpec(shape, lambda i: (0,) * len(shape))
    kern = functools.partial(_peer_select_kernel, n_heads=PEER_HEADS, topk=PEER_TOPK,
                             row_scale=ROWS_PER_EXPERT)
    return pl.pallas_call(
        kern,
        grid=(n_tok // tm,),
        in_specs=[pl.BlockSpec((tm, d), lambda i: (i, 0)), full(wq_bf.shape), full(sk_bf.shape)],
        out_specs=[pl.BlockSpec((n_slots, tm), lambda i: (0, i)),
                   pl.BlockSpec((n_slots, tm), lambda i: (0, i))],
        out_shape=[jax.ShapeDtypeStruct((n_slots, n_tok), jnp.int32),
                   jax.ShapeDtypeStruct((n_slots, n_tok), F32)],
        scratch_shapes=[pltpu.VMEM((2, PEER_TOPK, tm), F32), pltpu.VMEM((2, PEER_TOPK, tm), F32),
                        pltpu.VMEM((PEER_TOPK, tm), F32), pltpu.VMEM((PEER_TOPK, tm), F32)],
        compiler_params=pltpu.CompilerParams(
            dimension_semantics=("arbitrary",), vmem_limit_bytes=VMEM_LIMIT),
        name="peer_select",
    )(h2, wq_bf, sk_bf)


ROWS_PER_EXPERT = 4
PEER_TOKENS_PER_STEP = 32


def _pack_table(tab):
    n_exp, d = tab.shape
    assert d == 2 * ROWS_PER_EXPERT * LANES
    bits = lax.bitcast_convert_type(tab.astype(BF16), jnp.uint16).astype(jnp.uint32)
    bits = bits.reshape(n_exp, ROWS_PER_EXPERT, 2, LANES)
    words = bits[:, :, 0, :] | (bits[:, :, 1, :] << 16)
    return words.reshape(n_exp * ROWS_PER_EXPERT, LANES)


def _load_table(tbl_hbm, tbl_vmem, sem):
    @pl.when(pl.program_id(0) == 0)
    def _():
        cp = pltpu.make_async_copy(tbl_hbm, tbl_vmem, sem)
        cp.start()
        cp.wait()


def _gather_rows(idx_ref, t, tbl_vmem, tile_ref, n_slots):
    for k in range(n_slots):
        row = pl.multiple_of(idx_ref[t, k], ROWS_PER_EXPERT)
        tile_ref[k * ROWS_PER_EXPERT:(k + 1) * ROWS_PER_EXPERT, :] = tbl_vmem[pl.ds(row, ROWS_PER_EXPERT), :]


def _chunk_mask(width):
    r = lax.broadcasted_iota(jnp.int32, (SUBLANES, width), 0)
    j = lax.broadcasted_iota(jnp.int32, (SUBLANES, width), 1)
    return jnp.where(j % SUBLANES == r, 1.0, 0.0).astype(F32)


def _dot_exact_rhs(x, rhs_bf):
    rows = x.shape[0]
    p1 = x.astype(BF16)
    r1 = x - p1.astype(F32)
    p2 = r1.astype(BF16)
    p3 = (r1 - p2.astype(F32)).astype(BF16)
    res = jnp.dot(jnp.concatenate([p1, p2, p3], axis=0), rhs_bf, preferred_element_type=F32)
    return res[:rows] + res[rows:2 * rows] + res[2 * rows:]


def _peer_u_kernel(idx_ref, x_ref, tbl_hbm, pair_ref, hid_ref, tbl_vmem, tile_ref, hsc_ref, sem):
    _load_table(tbl_hbm, tbl_vmem, sem)
    tt = x_ref.shape[0]
    n_slots = idx_ref.shape[1]
    mask = _chunk_mask(n_slots * SUBLANES)
    for t in range(tt):
        tile = tile_ref.at[t % 2]
        _gather_rows(idx_ref, t, tbl_vmem, tile, n_slots)
        hi, lo = _split_bf16(x_ref[t])
        lhs = jnp.concatenate([hi, lo], axis=0)
        rhs = pltpu.bitcast(tile[...], BF16)
        out = lax.dot_general(lhs, rhs, (((1,), (1,)), ((), ())), preferred_element_type=F32)
        z = (out[0:SUBLANES] + out[SUBLANES:]) * mask
        hsc_ref[t:t + 1, :] = jnp.sum(z, axis=0, keepdims=True)
    hid_ref[...] = _dot_exact_rhs(hsc_ref[...], pair_ref[...])


def _peer_v_kernel(idx_ref, hid_ref, gate_ref, tbl_hbm, rep_ref, y_ref, tbl_vmem, tile_ref, wrep_ref, sem):
    _load_table(tbl_hbm, tbl_vmem, sem)
    tt = y_ref.shape[0]
    n_slots = idx_ref.shape[1]
    mask = _chunk_mask(n_slots * SUBLANES)
    w = gate_ref[...] * _gelu(hid_ref[...])
    wrep_ref[...] = _dot_exact_rhs(w, rep_ref[...])
    for t in range(tt):
        tile = tile_ref.at[t % 2]
        _gather_rows(idx_ref, t, tbl_vmem, tile, n_slots)
        hi, lo = _split_bf16(wrep_ref[t:t + 1, :] * mask)
        lhs = jnp.concatenate([hi, lo], axis=0)
        rhs = pltpu.bitcast(tile[...], BF16)
        out = jnp.dot(lhs, rhs, preferred_element_type=F32)
        y_ref[t] = out[0:SUBLANES] + out[SUBLANES:]


def _peer_scratch(tbl, tt, n_slots):
    return [pltpu.VMEM(tbl.shape, jnp.uint32),
            pltpu.VMEM((2, n_slots * ROWS_PER_EXPERT, LANES), jnp.uint32),
            pltpu.VMEM((tt, n_slots * SUBLANES), F32),
            pltpu.SemaphoreType.DMA(())]


def _peer_u(idx, h3, tbl, tt):
    n_tok, n_slots = idx.shape
    width = n_slots * SUBLANES
    pair = (np.arange(width)[:, None] // SUBLANES == np.arange(n_slots)[None, :]).astype(np.float32)
    return pl.pallas_call(
        _peer_u_kernel,
        grid=(n_tok // tt,),
        in_specs=[pl.BlockSpec((tt, n_slots), lambda i: (i, 0), memory_space=pltpu.SMEM),
                  pl.BlockSpec((tt, SUBLANES, LANES), lambda i: (i, 0, 0)),
                  pl.BlockSpec(memory_space=pl.ANY),
                  pl.BlockSpec((width, n_slots), lambda i: (0, 0))],
        out_specs=pl.BlockSpec((tt, n_slots), lambda i: (i, 0)),
        out_shape=jax.ShapeDtypeStruct((n_tok, n_slots), F32),
        scratch_shapes=_peer_scratch(tbl, tt, n_slots),
        compiler_params=pltpu.CompilerParams(
            dimension_semantics=("arbitrary",), vmem_limit_bytes=VMEM_LIMIT),
        name="peer_u",
    )(idx, h3, tbl, jnp.asarray(pair, dtype=BF16))


def _peer_v(idx, hid, gates, tbl, tt):
    n_tok, n_slots = idx.shape
    width = n_slots * SUBLANES
    rep = (np.arange(n_slots)[:, None] == np.arange(width)[None, :] // SUBLANES).astype(np.float32)
    tok = pl.BlockSpec((tt, n_slots), lambda i: (i, 0))
    return pl.pallas_call(
        _peer_v_kernel,
        grid=(n_tok // tt,),
        in_specs=[pl.BlockSpec((tt, n_slots), lambda i: (i, 0), memory_space=pltpu.SMEM),
                  tok, tok,
                  pl.BlockSpec(memory_space=pl.ANY),
                  pl.BlockSpec((n_slots, width), lambda i: (0, 0))],
        out_specs=pl.BlockSpec((tt, SUBLANES, LANES), lambda i: (i, 0, 0)),
        out_shape=jax.ShapeDtypeStruct((n_tok, SUBLANES, LANES), F32),
        scratch_shapes=_peer_scratch(tbl, tt, n_slots),
        compiler_params=pltpu.CompilerParams(
            dimension_semantics=("arbitrary",), vmem_limit_bytes=VMEM_LIMIT),
        name="peer_v",
    )(idx, hid, gates, tbl, jnp.asarray(rep, dtype=BF16))


def _final_kernel(h_ref, f_ref, p_ref, wple_ref, wpg_ref, bpg_ref, g_ref, b_ref, o_ref, *, alpha):
    h = h_ref[...]
    gate = jax.nn.sigmoid(jnp.dot(h.astype(BF16), wpg_ref[...], preferred_element_type=F32) + bpg_ref[...])
    e = jnp.dot(p_ref[...].astype(BF16), wple_ref[...], preferred_element_type=F32) * gate
    o_ref[...] = _layer_norm(alpha * h + f_ref[...] + e, g_ref[...], b_ref[...])


def _final(h2, f2, p2, wple_bf, wpg_bf, bpg, g2, b2, alpha, tm):
    n_tok, d = h2.shape
    pd = p2.shape[1]
    full = lambda shape: pl.BlockSpec(shape, lambda i: (0,) * len(shape))
    tok = pl.BlockSpec((tm, d), lambda i: (i, 0))
    return pl.pallas_call(
        functools.partial(_final_kernel, alpha=alpha),
        grid=(n_tok // tm,),
        in_specs=[tok, tok, pl.BlockSpec((tm, pd), lambda i: (i, 0)),
                  full((pd, d)), full((d, d)), full((1, d)), full((1, d)), full((1, d))],
        out_specs=tok,
        out_shape=jax.ShapeDtypeStruct((n_tok, d), F32),
        compiler_params=pltpu.CompilerParams(
            dimension_semantics=("arbitrary",), vmem_limit_bytes=VMEM_LIMIT),
        name="final",
    )(h2, f2, p2, wple_bf, wpg_bf, bpg, g2, b2)


def _token_tile(n_tok, want):
    tm = min(want, n_tok)
    assert n_tok % tm == 0
    return tm


def kernel(x, p, w_in, b_in, gmlp_ln_g, gmlp_ln_b, gmlp_w_s, gmlp_b_s, w_proj_a, w_proj_b, w_out,
           ln1_g, ln1_b, rel_bias, peer_w_q, peer_sub_keys, peer_u, peer_v, ple_w_proj, ple_w_gate,
           ple_b_gate, ln2_g, ln2_b):
    depth = w_in.shape[0]
    bsz, seq, d = x.shape
    n_tok = bsz * seq
    alpha = (2.0 * depth) ** 0.25
    assert d == GMLP_GROUPS * GMLP_CHUNK and seq % MOBA_BLOCK == 0
    row = lambda v: v.reshape(1, -1)
    bias = _bias_tiles(rel_bias, seq // MOBA_BLOCK, MOBA_BLOCK)
    x2 = x.reshape(n_tok, d)
    for i in range(depth):
        z = _in_proj(x2, w_in[i].astype(BF16), row(b_in[i]), _token_tile(n_tok, 512))
        o = _moba(z.reshape(z.shape[0], bsz, seq, d), bias, MOBA_HEADS)
        bs_b = jnp.broadcast_to(gmlp_b_s[i][:, :, None], gmlp_b_s[i].shape + (d // GMLP_GROUPS,))
        h2 = _mix(z, o.reshape(n_tok, d), x2, row(gmlp_ln_g[i]), row(gmlp_ln_b[i]),
                  jnp.tril(gmlp_w_s[i]).astype(BF16), bs_b,
                  w_proj_a[i].astype(BF16), w_proj_b[i].astype(BF16), w_out[i].astype(BF16),
                  row(ln1_g[i]), row(ln1_b[i]), alpha, _token_tile(n_tok, 256))
        experts_t, gates_t = _peer_select(h2, peer_w_q[i].astype(BF16), peer_sub_keys[i].astype(BF16),
                                          _token_tile(n_tok, 256))
        experts, gates = experts_t.T, gates_t.T
        tt = _token_tile(n_tok, PEER_TOKENS_PER_STEP)
        hid = _peer_u(experts, h2.reshape(n_tok, SUBLANES, LANES), _pack_table(peer_u[i]), tt)
        f = _peer_v(experts, hid, gates, _pack_table(peer_v[i]), tt)
        x2 = _final(h2, f.reshape(n_tok, d), p[i].reshape(n_tok, -1), ple_w_proj[i].astype(BF16),
                    ple_w_gate[i].astype(BF16), row(ple_b_gate[i]), row(ln2_g[i]), row(ln2_b[i]),
                    alpha, _token_tile(n_tok, 512))
    return x2.reshape(bsz, seq, d)
```

```python
import functools
import math

import numpy as np
import jax
import jax.numpy as jnp
from jax import lax
from jax.experimental import pallas as pl
from jax.experimental.pallas import tpu as pltpu

F32 = jnp.float32
BF16 = jnp.bfloat16
HIGHEST = lax.Precision.HIGHEST

LANES = 128
SUBLANES = 8
LN_EPS = 1e-5
NEG_INF = -1e30

GMLP_GROUPS = 8
GMLP_CHUNK = 128
MOBA_HEADS = 8
MOBA_BLOCK = 256
MOBA_TOPK = 3
REL_BUCKETS = 32
REL_MAX_DISTANCE = 1024
PEER_HEADS = 8
PEER_TOPK = 16

VMEM_LIMIT = 56 * 1024 * 1024


def _layer_norm(x, g, b):
    mu = jnp.mean(x, axis=-1, keepdims=True)
    xc = x - mu
    var = jnp.mean(xc * xc, axis=-1, keepdims=True)
    return xc * lax.rsqrt(var + LN_EPS) * g + b


def _gelu(x):
    return 0.5 * x * (1.0 + lax.erf(x * math.sqrt(0.5)))


def _split_bf16(x):
    hi = x.astype(BF16)
    lo = (x - hi.astype(F32)).astype(BF16)
    return hi, lo


def _inproj_kernel(x_ref, w_ref, b_ref, o_ref):
    xb = x_ref[...].astype(BF16)
    for seg in range(w_ref.shape[0]):
        acc = jnp.dot(xb, w_ref[seg], preferred_element_type=F32) + b_ref[seg]
        if seg < 2:
            o_ref[seg] = _gelu(acc)
        elif seg < 5:
            o_ref[seg] = acc
        else:
            o_ref[seg] = jax.nn.sigmoid(acc)


def _in_proj(x2, w_seg, b_seg, tm):
    n_tok, d = x2.shape
    n_seg = w_seg.shape[0]
    return pl.pallas_call(
        _inproj_kernel,
        grid=(n_tok // tm,),
        in_specs=[
            pl.BlockSpec((tm, d), lambda i: (i, 0)),
            pl.BlockSpec((n_seg, d, d), lambda i: (0, 0, 0)),
            pl.BlockSpec((n_seg, 1, d), lambda i: (0, 0, 0)),
        ],
        out_specs=pl.BlockSpec((n_seg, tm, d), lambda i: (0, i, 0)),
        out_shape=jax.ShapeDtypeStruct((n_seg, n_tok, d), F32),
        compiler_params=pltpu.CompilerParams(
            dimension_semantics=("arbitrary",), vmem_limit_bytes=VMEM_LIMIT),
        name="in_proj",
    )(x2, w_seg, b_seg)


def _moba_kernel(q_ref, k_ref, v_ref, bias_ref, o_ref, kb_scr, vb_scr, kmean_scr, *, scale):
    i = pl.program_id(2)
    seq, hd = kb_scr.shape
    blk = q_ref.shape[2]
    nb = seq // blk
    nbp = kmean_scr.shape[0]
    pair = 2 * blk

    @pl.when(i == 0)
    def _():
        kmean_scr[...] = jnp.zeros_like(kmean_scr)
        for n in range(nb):
            kf = k_ref[0, 0, n * blk:(n + 1) * blk, :]
            kmean_scr[n:n + 1, :] = jnp.mean(kf, axis=0, keepdims=True)
            kb_scr[n * blk:(n + 1) * blk, :] = kf.astype(BF16)
            vb_scr[n * blk:(n + 1) * blk, :] = v_ref[0, 0, n * blk:(n + 1) * blk, :].astype(BF16)

    q = q_ref[0, 0]

    gate = lax.dot_general(kmean_scr[...], q, (((1,), (1,)), ((), ())),
                           precision=HIGHEST, preferred_element_type=F32)
    n_iota = lax.broadcasted_iota(jnp.int32, gate.shape, 0)
    past = n_iota < i
    g = jnp.where(past, gate, NEG_INF)
    picked = jnp.zeros(gate.shape, F32)
    for _ in range(MOBA_TOPK):
        m = jnp.max(g, axis=0, keepdims=True)
        first = jnp.min(jnp.where(g == m, n_iota, nbp), axis=0, keepdims=True)
        hit = n_iota == first
        picked = jnp.where(hit, 1.0, picked)
        g = jnp.where(hit, -jnp.inf, g)
    allowed = jnp.logical_or(jnp.logical_and(picked > 0.5, past), n_iota == i)
    sel_neg = jnp.where(allowed, 0.0, NEG_INF)
    sel_pad = jnp.concatenate([sel_neg, jnp.full((LANES - nbp, blk), NEG_INF, F32)], axis=0)
    q_aug = jnp.concatenate([(q * scale).astype(BF16), sel_pad.T.astype(BF16)], axis=1)

    row = lax.broadcasted_iota(jnp.int32, (blk, blk), 0)
    col = lax.broadcasted_iota(jnp.int32, (blk, blk), 1)
    tri = row >= col
    lane_blk = lax.broadcasted_iota(jnp.int32, (pair, LANES), 1)
    second = (lax.broadcasted_iota(jnp.int32, (pair, LANES), 0) >= blk).astype(jnp.int32)
    n_pairs = (i + 2) // 2

    def block_pair(it, carry):
        m_run, l_run, acc = carry
        n0 = 2 * (n_pairs - 1 - it)
        start = pl.multiple_of(n0 * blk, pair)
        onehot = jnp.where(lane_blk == n0 + second, 1.0, 0.0).astype(BF16)
        k_aug = jnp.concatenate([kb_scr[pl.ds(start, pair), :], onehot], axis=1)
        s = lax.dot_general(q_aug, k_aug, (((1,), (1,)), ((), ())), preferred_element_type=F32)
        s0 = s[:, :blk] + bias_ref[0, i - n0]
        s1 = s[:, blk:] + bias_ref[0, jnp.maximum(i - n0 - 1, 0)]
        s0 = jnp.where(jnp.logical_or(tri, n0 < i), s0, NEG_INF)
        s1 = jnp.where(jnp.logical_or(tri, n0 + 1 < i), s1, NEG_INF)
        s = jnp.concatenate([s0, s1], axis=1)
        m_new = jnp.maximum(m_run, jnp.max(s, axis=1, keepdims=True))
        corr = jnp.exp(m_run - m_new)
        p = jnp.exp(s - m_new)
        l_new = corr * l_run + jnp.sum(p, axis=1, keepdims=True)
        acc_new = corr * acc + jnp.dot(p.astype(BF16), vb_scr[pl.ds(start, pair), :],
                                       preferred_element_type=F32)
        return m_new, l_new, acc_new

    init = (jnp.full((blk, 1), NEG_INF, F32), jnp.zeros((blk, 1), F32), jnp.zeros((blk, hd), F32))
    _, l_fin, acc_fin = lax.fori_loop(0, n_pairs, block_pair, init)
    o_ref[0] = acc_fin / l_fin


def _t5_bucket(dist):
    n = jnp.maximum(dist, 0)
    max_exact = REL_BUCKETS // 2
    nf = jnp.maximum(n, max_exact).astype(jnp.float32)
    large = max_exact + (jnp.log(nf / max_exact) / math.log(REL_MAX_DISTANCE / max_exact)
                         * (REL_BUCKETS - max_exact)).astype(jnp.int32)
    large = jnp.minimum(large, REL_BUCKETS - 1)
    return jnp.where(n < max_exact, n, large)


def _bias_tiles(rel_bias, nb, blk):
    n_heads = rel_bias.shape[1]
    dist = jnp.arange(-(blk - 1), nb * blk, dtype=jnp.int32)
    bvec = rel_bias.T.astype(F32)[:, _t5_bucket(dist)]
    period = 2 * blk - 1
    win = jnp.stack([bvec[:, d * blk:d * blk + period] for d in range(nb)], axis=1)
    rev = win[..., ::-1]
    tiled = jnp.tile(rev, (1, 1, blk + 1))[..., :2 * blk * blk]
    rows = tiled.reshape(n_heads, nb, blk, 2 * blk)[..., :blk]
    return rows[:, :, ::-1, :]


def _moba(z4, bias, n_heads):
    _, bsz, seq, d = z4.shape
    hd = d // n_heads
    blk = MOBA_BLOCK
    nb = seq // blk
    assert nb <= LANES and nb % 2 == 0 and seq % blk == 0
    kern = functools.partial(_moba_kernel, scale=hd ** -0.5)
    return pl.pallas_call(
        kern,
        grid=(bsz, n_heads, nb),
        in_specs=[
            pl.BlockSpec((1, 1, blk, hd), lambda b, h, i: (2, b, i, h)),
            pl.BlockSpec((1, 1, seq, hd), lambda b, h, i: (3, b, 0, h)),
            pl.BlockSpec((1, 1, seq, hd), lambda b, h, i: (4, b, 0, h)),
            pl.BlockSpec((1, nb, blk, blk), lambda b, h, i: (h, 0, 0, 0)),
        ],
        out_specs=pl.BlockSpec((1, blk, hd), lambda b, h, i: (b, i, h)),
        out_shape=jax.ShapeDtypeStruct((bsz, seq, d), F32),
        scratch_shapes=[
            pltpu.VMEM((seq, hd), BF16),
            pltpu.VMEM((seq, hd), BF16),
            pltpu.VMEM((-(-nb // SUBLANES) * SUBLANES, hd), F32),
        ],
        compiler_params=pltpu.CompilerParams(
            dimension_semantics=("arbitrary", "arbitrary", "arbitrary"),
            vmem_limit_bytes=VMEM_LIMIT),
        name="moba",
    )(z4, z4, z4, bias)


def _mix_kernel(ug_ref, vg_ref, sa_ref, sb_ref, o_ref, x_ref, lng_ref, lnb_ref, ws_ref, bs_ref,
                pa_ref, pb_ref, wo_ref, g1_ref, b1_ref, h_ref, *, alpha):
    tm, d = x_ref.shape
    gw = d // GMLP_GROUPS
    vb = _layer_norm(vg_ref[0], lng_ref[...], lnb_ref[...]).astype(BF16)
    rows = []
    for n in range(tm // GMLP_CHUNK):
        cols = []
        for g in range(GMLP_GROUPS):
            vc = vb[n * GMLP_CHUNK:(n + 1) * GMLP_CHUNK, g * gw:(g + 1) * gw]
            cols.append(jnp.dot(ws_ref[g], vc, preferred_element_type=F32) + bs_ref[g])
        rows.append(jnp.concatenate(cols, axis=1))
    a = ug_ref[0] * jnp.concatenate(rows, axis=0)
    ma = jnp.dot(a.astype(BF16), pa_ref[...], preferred_element_type=F32)
    mb = jnp.dot(o_ref[...].astype(BF16), pb_ref[...], preferred_element_type=F32)
    m = sa_ref[0] * ma + sb_ref[0] * mb
    y = alpha * x_ref[...] + jnp.dot(m.astype(BF16), wo_ref[...], preferred_element_type=F32)
    h_ref[...] = _layer_norm(y, g1_ref[...], b1_ref[...])


def _mix(z, o2, x2, lng, lnb, ws_bf, bs_b, pa_bf, pb_bf, wo_bf, g1, b1, alpha, tm):
    n_tok, d = x2.shape
    full = lambda shape: pl.BlockSpec(shape, lambda i: (0,) * len(shape))
    seg = lambda s: pl.BlockSpec((1, tm, d), lambda i, s=s: (s, i, 0))
    tok = pl.BlockSpec((tm, d), lambda i: (i, 0))
    return pl.pallas_call(
        functools.partial(_mix_kernel, alpha=alpha),
        grid=(n_tok // tm,),
        in_specs=[seg(0), seg(1), seg(5), seg(6), tok, tok,
                  full((1, d)), full((1, d)), full(ws_bf.shape), full(bs_b.shape),
                  full((d, d)), full((d, d)), full((d, d)), full((1, d)), full((1, d))],
        out_specs=tok,
        out_shape=jax.ShapeDtypeStruct((n_tok, d), F32),
        compiler_params=pltpu.CompilerParams(
            dimension_semantics=("arbitrary",), vmem_limit_bytes=VMEM_LIMIT),
        name="mix",
    )(z, z, z, z, o2, x2, lng, lnb, ws_bf, bs_b, pa_bf, pb_bf, wo_bf, g1, b1)


def _staircase_pieces(topk):
    pieces = [("col", 0, topk)]
    for b in range(1, SUBLANES):
        pieces.append(("col", b, SUBLANES))
    pieces.append(("row", SUBLANES, topk - SUBLANES))
    return pieces


def _peer_select_kernel(h_ref, wq_ref, sk_ref, exp_ref, gate_ref, ts_scr, ti_scr, bs_scr, be_scr,
                        gt_scr, et_scr, *, n_heads, topk, row_scale):
    tm = h_ref.shape[0]
    nk, kh = sk_ref.shape[1], sk_ref.shape[2]
    q = jnp.dot(h_ref[...].astype(BF16), wq_ref[...], preferred_element_type=F32)
    qb = q.astype(BF16)
    key_iota = lax.broadcasted_iota(jnp.int32, (nk, tm), 0).astype(F32)

    pieces = _staircase_pieces(topk)
    flat_parts, valid_parts = [], []
    for kind, b, rows in pieces:
        r_iota = lax.broadcasted_iota(jnp.int32, (rows, tm), 0)
        if kind == "col":
            flat_parts.append(r_iota * topk + b)
            valid_parts.append(r_iota < topk // (b + 1))
        else:
            flat_parts.append(r_iota + b)
            valid_parts.append(r_iota >= 0)
    flat = jnp.concatenate(flat_parts, axis=0).astype(F32)
    valid = jnp.concatenate(valid_parts, axis=0)
    big = float(topk * topk)

    for hh in range(n_heads):
        for half in range(2):
            c0 = (hh * 2 + half) * kh
            s = lax.dot_general(sk_ref[half], qb[:, c0:c0 + kh], (((1,), (1,)), ((), ())),
                                preferred_element_type=F32)
            for r in range(topk):
                m = jnp.max(s, axis=0, keepdims=True)
                idx = jnp.min(jnp.where(s == m, key_iota, float(nk)), axis=0, keepdims=True)
                ts_scr[half, r:r + 1, :] = m
                ti_scr[half, r:r + 1, :] = idx
                s = jnp.where(key_iota == idx, -jnp.inf, s)
        t0, t1 = ts_scr[0], ts_scr[1]
        i0, i1 = ti_scr[0], ti_scr[1]
        cs, ce = [], []
        for kind, b, rows in pieces:
            if kind == "col":
                cs.append(t0[0:rows] + t1[b:b + 1])
                ce.append(i0[0:rows] * float(nk) + i1[b:b + 1])
            else:
                cs.append(t0[0:1] + t1[b:b + rows])
                ce.append(i0[0:1] * float(nk) + i1[b:b + rows])
        cand = jnp.where(valid, jnp.concatenate(cs, axis=0), -jnp.inf)
        cexp = jnp.concatenate(ce, axis=0)
        for r in range(topk):
            m = jnp.max(cand, axis=0, keepdims=True)
            fsel = jnp.min(jnp.where(cand == m, flat, big), axis=0, keepdims=True)
            hit = flat == fsel
            bs_scr[r:r + 1, :] = m
            be_scr[r:r + 1, :] = jnp.max(jnp.where(hit, cexp, -1.0), axis=0, keepdims=True)
            cand = jnp.where(hit, -jnp.inf, cand)
        best = bs_scr[...]
        ex = jnp.exp(best - jnp.max(best, axis=0, keepdims=True))
        gt_scr[hh * topk:(hh + 1) * topk, :] = ex / jnp.sum(ex, axis=0, keepdims=True)
        et_scr[hh * topk:(hh + 1) * topk, :] = be_scr[...] * float(row_scale)
    gate_ref[...] = gt_scr[...].T
    exp_ref[...] = et_scr[...].T.astype(jnp.int32)


def _peer_select(h2, wq_bf, sk_bf, tm):
    n_tok, d = h2.shape
    n_slots = PEER_HEADS * PEER_TOPK
    full = lambda shape: pl.BlockSpec(shape, lambda i: (0,) * len(shape))
    kern = functools.partial(_peer_select_kernel, n_heads=PEER_HEADS, topk=PEER_TOPK,
                             row_scale=ROWS_PER_EXPERT)
    return pl.pallas_call(
        kern,
        grid=(n_tok // tm,),
        in_specs=[pl.BlockSpec((tm, d), lambda i: (i, 0)), full(wq_bf.shape), full(sk_bf.shape)],
        out_specs=[pl.BlockSpec((tm, n_slots), lambda i: (i, 0)),
                   pl.BlockSpec((tm, n_slots), lambda i: (i, 0))],
        out_shape=[jax.ShapeDtypeStruct((n_tok, n_slots), jnp.int32),
                   jax.ShapeDtypeStruct((n_tok, n_slots), F32)],
        scratch_shapes=[pltpu.VMEM((2, PEER_TOPK, tm), F32), pltpu.VMEM((2, PEER_TOPK, tm), F32),
                        pltpu.VMEM((PEER_TOPK, tm), F32), pltpu.VMEM((PEER_TOPK, tm), F32),
                        pltpu.VMEM((n_slots, tm), F32), pltpu.VMEM((n_slots, tm), F32)],
        compiler_params=pltpu.CompilerParams(
            dimension_semantics=("arbitrary",), vmem_limit_bytes=VMEM_LIMIT),
        name="peer_select",
    )(h2, wq_bf, sk_bf)


ROWS_PER_EXPERT = 4
PEER_IDX_ROWS = 16
PEER_TOKENS_PER_STEP = 2 * PEER_IDX_ROWS


def _pack_table(tab):
    n_exp, d = tab.shape
    assert d == 2 * ROWS_PER_EXPERT * LANES
    bits = lax.bitcast_convert_type(tab.astype(BF16), jnp.uint16).astype(jnp.uint32)
    bits = bits.reshape(n_exp, ROWS_PER_EXPERT, 2, LANES)
    words = bits[:, :, 0, :] | (bits[:, :, 1, :] << 16)
    return words.reshape(n_exp * ROWS_PER_EXPERT, LANES)


def _idx_copy(idx_hbm, buf, sem, step, half):
    first = (step * 2 + half) * PEER_IDX_ROWS
    return pltpu.make_async_copy(idx_hbm.at[pl.ds(first, PEER_IDX_ROWS)], buf, sem)


def _peer_tokens(idx_hbm, idx_bufs, idx_sems, tbl_hbm, tbl_vmem, tbl_sem, tile_ref, token_fn):
    step = pl.program_id(0)
    n_steps = pl.num_programs(0)
    n_slots = idx_hbm.shape[1]

    @pl.when(step == 0)
    def _():
        for half in range(2):
            _idx_copy(idx_hbm, idx_bufs[half], idx_sems.at[half], 0, half).start()
        cp = pltpu.make_async_copy(tbl_hbm, tbl_vmem, tbl_sem)
        cp.start()
        cp.wait()

    for half in range(2):
        _idx_copy(idx_hbm, idx_bufs[half], idx_sems.at[half], step, half).wait()
        for r in range(PEER_IDX_ROWS):
            t = half * PEER_IDX_ROWS + r
            tile = tile_ref.at[t % 2]
            for k in range(n_slots):
                row = pl.multiple_of(idx_bufs[half][r, k], ROWS_PER_EXPERT)
                tile[k * ROWS_PER_EXPERT:(k + 1) * ROWS_PER_EXPERT, :] = tbl_vmem[pl.ds(row, ROWS_PER_EXPERT), :]
            token_fn(t, tile)

        @pl.when(step + 1 < n_steps)
        def _():
            _idx_copy(idx_hbm, idx_bufs[half], idx_sems.at[half], step + 1, half).start()


def _chunk_mask(width):
    r = lax.broadcasted_iota(jnp.int32, (SUBLANES, width), 0)
    j = lax.broadcasted_iota(jnp.int32, (SUBLANES, width), 1)
    return jnp.where(j % SUBLANES == r, 1.0, 0.0).astype(F32)


def _dot_exact_rhs(x, rhs_bf):
    rows = x.shape[0]
    p1 = x.astype(BF16)
    r1 = x - p1.astype(F32)
    p2 = r1.astype(BF16)
    p3 = (r1 - p2.astype(F32)).astype(BF16)
    res = jnp.dot(jnp.concatenate([p1, p2, p3], axis=0), rhs_bf, preferred_element_type=F32)
    return res[:rows] + res[rows:2 * rows] + res[2 * rows:]


def _peer_u_kernel(idx_hbm, x_ref, tbl_hbm, pair_ref, hid_ref, tbl_vmem, tile_ref, hsc_ref, idx_a, idx_b,
                   tbl_sem, idx_sems):
    mask = _chunk_mask(hsc_ref.shape[1])

    def token(t, tile):
        hi, lo = _split_bf16(x_ref[t])
        lhs = jnp.concatenate([hi, lo], axis=0)
        rhs = pltpu.bitcast(tile[...], BF16)
        out = lax.dot_general(lhs, rhs, (((1,), (1,)), ((), ())), preferred_element_type=F32)
        z = (out[0:SUBLANES] + out[SUBLANES:]) * mask
        hsc_ref[t:t + 1, :] = jnp.sum(z, axis=0, keepdims=True)

    _peer_tokens(idx_hbm, (idx_a, idx_b), idx_sems, tbl_hbm, tbl_vmem, tbl_sem, tile_ref, token)
    hid_ref[...] = _dot_exact_rhs(hsc_ref[...], pair_ref[...])


def _peer_v_kernel(idx_hbm, hid_ref, gate_ref, tbl_hbm, rep_ref, y_ref, tbl_vmem, tile_ref, wrep_ref, idx_a,
                   idx_b, tbl_sem, idx_sems):
    mask = _chunk_mask(wrep_ref.shape[1])
    w = gate_ref[...] * _gelu(hid_ref[...])
    wrep_ref[...] = _dot_exact_rhs(w, rep_ref[...])

    def token(t, tile):
        hi, lo = _split_bf16(wrep_ref[t:t + 1, :] * mask)
        lhs = jnp.concatenate([hi, lo], axis=0)
        rhs = pltpu.bitcast(tile[...], BF16)
        out = jnp.dot(lhs, rhs, preferred_element_type=F32)
        y_ref[t] = out[0:SUBLANES] + out[SUBLANES:]

    _peer_tokens(idx_hbm, (idx_a, idx_b), idx_sems, tbl_hbm, tbl_vmem, tbl_sem, tile_ref, token)


def _peer_scratch(tbl, tt, n_slots):
    assert tt == 2 * PEER_IDX_ROWS
    return [pltpu.VMEM(tbl.shape, jnp.uint32),
            pltpu.VMEM((2, n_slots * ROWS_PER_EXPERT, LANES), jnp.uint32),
            pltpu.VMEM((tt, n_slots * SUBLANES), F32),
            pltpu.SMEM((PEER_IDX_ROWS, n_slots), jnp.int32),
            pltpu.SMEM((PEER_IDX_ROWS, n_slots), jnp.int32),
            pltpu.SemaphoreType.DMA(()),
            pltpu.SemaphoreType.DMA((2,))]


def _peer_u(idx, h3, tbl, tt):
    n_tok, n_slots = idx.shape
    width = n_slots * SUBLANES
    pair = (np.arange(width)[:, None] // SUBLANES == np.arange(n_slots)[None, :]).astype(np.float32)
    return pl.pallas_call(
        _peer_u_kernel,
        grid=(n_tok // tt,),
        in_specs=[pl.BlockSpec(memory_space=pl.ANY),
                  pl.BlockSpec((tt, SUBLANES, LANES), lambda i: (i, 0, 0)),
                  pl.BlockSpec(memory_space=pl.ANY),
                  pl.BlockSpec((width, n_slots), lambda i: (0, 0))],
        out_specs=pl.BlockSpec((tt, n_slots), lambda i: (i, 0)),
        out_shape=jax.ShapeDtypeStruct((n_tok, n_slots), F32),
        scratch_shapes=_peer_scratch(tbl, tt, n_slots),
        compiler_params=pltpu.CompilerParams(
            dimension_semantics=("arbitrary",), vmem_limit_bytes=VMEM_LIMIT),
        name="peer_u",
    )(idx, h3, tbl, jnp.asarray(pair, dtype=BF16))


def _peer_v(idx, hid, gates, tbl, tt):
    n_tok, n_slots = idx.shape
    width = n_slots * SUBLANES
    rep = (np.arange(n_slots)[:, None] == np.arange(width)[None, :] // SUBLANES).astype(np.float32)
    tok = pl.BlockSpec((tt, n_slots), lambda i: (i, 0))
    return pl.pallas_call(
        _peer_v_kernel,
        grid=(n_tok // tt,),
        in_specs=[pl.BlockSpec(memory_space=pl.ANY),
                  tok, tok,
                  pl.BlockSpec(memory_space=pl.ANY),
                  pl.BlockSpec((n_slots, width), lambda i: (0, 0))],
        out_specs=pl.BlockSpec((tt, SUBLANES, LANES), lambda i: (i, 0, 0)),
        out_shape=jax.ShapeDtypeStruct((n_tok, SUBLANES, LANES), F32),
        scratch_shapes=_peer_scratch(tbl, tt, n_slots),
        compiler_params=pltpu.CompilerParams(
            dimension_semantics=("arbitrary",), vmem_limit_bytes=VMEM_LIMIT),
        name="peer_v",
    )(idx, hid, gates, tbl, jnp.asarray(rep, dtype=BF16))


def _final_kernel(h_ref, f_ref, p_ref, wple_ref, wpg_ref, bpg_ref, g_ref, b_ref, o_ref, *, alpha):
    h = h_ref[...]
    gate = jax.nn.sigmoid(jnp.dot(h.astype(BF16), wpg_ref[...], preferred_element_type=F32) + bpg_ref[...])
    e = jnp.dot(p_ref[...].astype(BF16), wple_ref[...], preferred_element_type=F32) * gate
    o_ref[...] = _layer_norm(alpha * h + f_ref[...] + e, g_ref[...], b_ref[...])


def _final(h2, f2, p2, wple_bf, wpg_bf, bpg, g2, b2, alpha, tm):
    n_tok, d = h2.shape
    pd = p2.shape[1]
    full = lambda shape: pl.BlockSpec(shape, lambda i: (0,) * len(shape))
    tok = pl.BlockSpec((tm, d), lambda i: (i, 0))
    return pl.pallas_call(
        functools.partial(_final_kernel, alpha=alpha),
        grid=(n_tok // tm,),
        in_specs=[tok, tok, pl.BlockSpec((tm, pd), lambda i: (i, 0)),
                  full((pd, d)), full((d, d)), full((1, d)), full((1, d)), full((1, d))],
        out_specs=tok,
        out_shape=jax.ShapeDtypeStruct((n_tok, d), F32),
        compiler_params=pltpu.CompilerParams(
            dimension_semantics=("arbitrary",), vmem_limit_bytes=VMEM_LIMIT),
        name="final",
    )(h2, f2, p2, wple_bf, wpg_bf, bpg, g2, b2)


def _token_tile(n_tok, want):
    tm = min(want, n_tok)
    assert n_tok % tm == 0
    return tm


def kernel(x, p, w_in, b_in, gmlp_ln_g, gmlp_ln_b, gmlp_w_s, gmlp_b_s, w_proj_a, w_proj_b, w_out,
           ln1_g, ln1_b, rel_bias, peer_w_q, peer_sub_keys, peer_u, peer_v, ple_w_proj, ple_w_gate,
           ple_b_gate, ln2_g, ln2_b):
    depth = w_in.shape[0]
    bsz, seq, d = x.shape
    n_tok = bsz * seq
    alpha = (2.0 * depth) ** 0.25
    assert d == GMLP_GROUPS * GMLP_CHUNK and seq % MOBA_BLOCK == 0
    row = lambda v: v.reshape(1, -1)
    bias = _bias_tiles(rel_bias, seq // MOBA_BLOCK, MOBA_BLOCK)
    x2 = x.reshape(n_tok, d)
    for i in range(depth):
        n_seg = w_in.shape[2] // d
        w_seg = w_in[i].astype(BF16).reshape(d, n_seg, d).transpose(1, 0, 2)
        z = _in_proj(x2, w_seg, b_in[i].reshape(n_seg, 1, d), _token_tile(n_tok, 256))
        o = _moba(z.reshape(z.shape[0], bsz, seq, d), bias, MOBA_HEADS)
        bs_b = jnp.broadcast_to(gmlp_b_s[i][:, :, None], gmlp_b_s[i].shape + (d // GMLP_GROUPS,))
        h2 = _mix(z, o.reshape(n_tok, d), x2, row(gmlp_ln_g[i]), row(gmlp_ln_b[i]),
                  jnp.tril(gmlp_w_s[i]).astype(BF16), bs_b,
                  w_proj_a[i].astype(BF16), w_proj_b[i].astype(BF16), w_out[i].astype(BF16),
                  row(ln1_g[i]), row(ln1_b[i]), alpha, _token_tile(n_tok, 256))
        experts, gates = _peer_select(h2, peer_w_q[i].astype(BF16), peer_sub_keys[i].astype(BF16),
                                      _token_tile(n_tok, 256))
        tt = _token_tile(n_tok, PEER_TOKENS_PER_STEP)
        hid = _peer_u(experts, h2.reshape(n_tok, SUBLANES, LANES), _pack_table(peer_u[i]), tt)
        f = _peer_v(experts, hid, gates, _pack_table(peer_v[i]), tt)
        x2 = _final(h2, f.reshape(n_tok, d), p[i].reshape(n_tok, -1), ple_w_proj[i].astype(BF16),
                    ple_w_gate[i].astype(BF16), row(ple_b_gate[i]), row(ln2_g[i]), row(ln2_b[i]),
                    alpha, _token_tile(n_tok, 512))
    return x2.reshape(bsz, seq, d)
```

```python
import functools
import math

import numpy as np
import jax
import jax.numpy as jnp
from jax import lax
from jax.experimental import pallas as pl
from jax.experimental.pallas import tpu as pltpu

F32 = jnp.float32
BF16 = jnp.bfloat16
HIGHEST = lax.Precision.HIGHEST

LANES = 128
SUBLANES = 8
LN_EPS = 1e-5
NEG_INF = -1e30

GMLP_GROUPS = 8
GMLP_CHUNK = 128
MOBA_HEADS = 8
MOBA_BLOCK = 256
MOBA_TOPK = 3
REL_BUCKETS = 32
REL_MAX_DISTANCE = 1024
PEER_HEADS = 8
PEER_TOPK = 16

VMEM_LIMIT = 56 * 1024 * 1024


def _layer_norm(x, g, b):
    mu = jnp.mean(x, axis=-1, keepdims=True)
    xc = x - mu
    var = jnp.mean(xc * xc, axis=-1, keepdims=True)
    return xc * lax.rsqrt(var + LN_EPS) * g + b


def _gelu(x):
    return 0.5 * x * (1.0 + lax.erf(x * math.sqrt(0.5)))


def _split_bf16(x):
    hi = x.astype(BF16)
    lo = (x - hi.astype(F32)).astype(BF16)
    return hi, lo


def _inproj_kernel(x_ref, w_ref, b_ref, o_ref):
    xb = x_ref[...].astype(BF16)
    for seg in range(w_ref.shape[0]):
        acc = jnp.dot(xb, w_ref[seg], preferred_element_type=F32) + b_ref[seg]
        if seg < 2:
            o_ref[seg] = _gelu(acc)
        elif seg < 5:
            o_ref[seg] = acc
        else:
            o_ref[seg] = jax.nn.sigmoid(acc)


def _in_proj(x2, w_seg, b_seg, tm):
    n_tok, d = x2.shape
    n_seg = w_seg.shape[0]
    return pl.pallas_call(
        _inproj_kernel,
        grid=(n_tok // tm,),
        in_specs=[
            pl.BlockSpec((tm, d), lambda i: (i, 0)),
            pl.BlockSpec((n_seg, d, d), lambda i: (0, 0, 0)),
            pl.BlockSpec((n_seg, 1, d), lambda i: (0, 0, 0)),
        ],
        out_specs=pl.BlockSpec((n_seg, tm, d), lambda i: (0, i, 0)),
        out_shape=jax.ShapeDtypeStruct((n_seg, n_tok, d), F32),
        compiler_params=pltpu.CompilerParams(
            dimension_semantics=("arbitrary",), vmem_limit_bytes=VMEM_LIMIT),
        name="in_proj",
    )(x2, w_seg, b_seg)


def _moba_kernel(q_ref, k_ref, v_ref, bias_ref, o_ref, kb_scr, vb_scr, kmean_scr, *, scale):
    i = pl.program_id(2)
    seq, hd = kb_scr.shape
    blk = q_ref.shape[2]
    nb = seq // blk
    nbp = kmean_scr.shape[0]
    pair = 2 * blk

    @pl.when(i == 0)
    def _():
        kmean_scr[...] = jnp.zeros_like(kmean_scr)
        for n in range(nb):
            kf = k_ref[0, 0, n * blk:(n + 1) * blk, :]
            kmean_scr[n:n + 1, :] = jnp.mean(kf, axis=0, keepdims=True)
            kb_scr[n * blk:(n + 1) * blk, :] = kf.astype(BF16)
            vb_scr[n * blk:(n + 1) * blk, :] = v_ref[0, 0, n * blk:(n + 1) * blk, :].astype(BF16)

    q = q_ref[0, 0]

    gate = lax.dot_general(kmean_scr[...], q, (((1,), (1,)), ((), ())),
                           precision=HIGHEST, preferred_element_type=F32)
    n_iota = lax.broadcasted_iota(jnp.int32, gate.shape, 0)
    past = n_iota < i
    g = jnp.where(past, gate, NEG_INF)
    picked = jnp.zeros(gate.shape, F32)
    for _ in range(MOBA_TOPK):
        m = jnp.max(g, axis=0, keepdims=True)
        first = jnp.min(jnp.where(g == m, n_iota, nbp), axis=0, keepdims=True)
        hit = n_iota == first
        picked = jnp.where(hit, 1.0, picked)
        g = jnp.where(hit, -jnp.inf, g)
    allowed = jnp.logical_or(jnp.logical_and(picked > 0.5, past), n_iota == i)
    sel_neg = jnp.where(allowed, 0.0, NEG_INF)
    sel_pad = jnp.concatenate([sel_neg, jnp.full((LANES - nbp, blk), NEG_INF, F32)], axis=0)
    q_aug = jnp.concatenate([(q * scale).astype(BF16), sel_pad.T.astype(BF16)], axis=1)

    row = lax.broadcasted_iota(jnp.int32, (blk, blk), 0)
    col = lax.broadcasted_iota(jnp.int32, (blk, blk), 1)
    tri = row >= col
    lane_blk = lax.broadcasted_iota(jnp.int32, (pair, LANES), 1)
    second = (lax.broadcasted_iota(jnp.int32, (pair, LANES), 0) >= blk).astype(jnp.int32)
    n_pairs = (i + 2) // 2

    def block_pair(it, carry):
        m_run, l_run, acc = carry
        n0 = 2 * (n_pairs - 1 - it)
        start = pl.multiple_of(n0 * blk, pair)
        onehot = jnp.where(lane_blk == n0 + second, 1.0, 0.0).astype(BF16)
        k_aug = jnp.concatenate([kb_scr[pl.ds(start, pair), :], onehot], axis=1)
        s = lax.dot_general(q_aug, k_aug, (((1,), (1,)), ((), ())), preferred_element_type=F32)
        s0 = s[:, :blk] + bias_ref[0, i - n0]
        s1 = s[:, blk:] + bias_ref[0, jnp.maximum(i - n0 - 1, 0)]
        s0 = jnp.where(jnp.logical_or(tri, n0 < i), s0, NEG_INF)
        s1 = jnp.where(jnp.logical_or(tri, n0 + 1 < i), s1, NEG_INF)
        s = jnp.concatenate([s0, s1], axis=1)
        m_new = jnp.maximum(m_run, jnp.max(s, axis=1, keepdims=True))
        corr = jnp.exp(m_run - m_new)
        p = jnp.exp(s - m_new)
        l_new = corr * l_run + jnp.sum(p, axis=1, keepdims=True)
        acc_new = corr * acc + jnp.dot(p.astype(BF16), vb_scr[pl.ds(start, pair), :],
                                       preferred_element_type=F32)
        return m_new, l_new, acc_new

    init = (jnp.full((blk, 1), NEG_INF, F32), jnp.zeros((blk, 1), F32), jnp.zeros((blk, hd), F32))
    _, l_fin, acc_fin = lax.fori_loop(0, n_pairs, block_pair, init)
    o_ref[0] = acc_fin / l_fin


def _t5_bucket(dist):
    n = jnp.maximum(dist, 0)
    max_exact = REL_BUCKETS // 2
    nf = jnp.maximum(n, max_exact).astype(jnp.float32)
    large = max_exact + (jnp.log(nf / max_exact) / math.log(REL_MAX_DISTANCE / max_exact)
                         * (REL_BUCKETS - max_exact)).astype(jnp.int32)
    large = jnp.minimum(large, REL_BUCKETS - 1)
    return jnp.where(n < max_exact, n, large)


def _bias_tiles(rel_bias, nb, blk):
    n_heads = rel_bias.shape[1]
    dist = jnp.arange(-(blk - 1), nb * blk, dtype=jnp.int32)
    bvec = rel_bias.T.astype(F32)[:, _t5_bucket(dist)]
    period = 2 * blk - 1
    win = jnp.stack([bvec[:, d * blk:d * blk + period] for d in range(nb)], axis=1)
    rev = win[..., ::-1]
    tiled = jnp.tile(rev, (1, 1, blk + 1))[..., :2 * blk * blk]
    rows = tiled.reshape(n_heads, nb, blk, 2 * blk)[..., :blk]
    return rows[:, :, ::-1, :]


def _moba(z4, bias, n_heads):
    _, bsz, seq, d = z4.shape
    hd = d // n_heads
    blk = MOBA_BLOCK
    nb = seq // blk
    assert nb <= LANES and nb % 2 == 0 and seq % blk == 0
    kern = functools.partial(_moba_kernel, scale=hd ** -0.5)
    return pl.pallas_call(
        kern,
        grid=(bsz, n_heads, nb),
        in_specs=[
            pl.BlockSpec((1, 1, blk, hd), lambda b, h, i: (2, b, i, h)),
            pl.BlockSpec((1, 1, seq, hd), lambda b, h, i: (3, b, 0, h)),
            pl.BlockSpec((1, 1, seq, hd), lambda b, h, i: (4, b, 0, h)),
            pl.BlockSpec((1, nb, blk, blk), lambda b, h, i: (h, 0, 0, 0)),
        ],
        out_specs=pl.BlockSpec((1, blk, hd), lambda b, h, i: (b, i, h)),
        out_shape=jax.ShapeDtypeStruct((bsz, seq, d), F32),
        scratch_shapes=[
            pltpu.VMEM((seq, hd), BF16),
            pltpu.VMEM((seq, hd), BF16),
            pltpu.VMEM((-(-nb // SUBLANES) * SUBLANES, hd), F32),
        ],
        compiler_params=pltpu.CompilerParams(
            dimension_semantics=("arbitrary", "arbitrary", "arbitrary"),
            vmem_limit_bytes=VMEM_LIMIT),
        name="moba",
    )(z4, z4, z4, bias)


def _mix_kernel(ug_ref, vg_ref, sa_ref, sb_ref, o_ref, x_ref, lng_ref, lnb_ref, ws_ref, bs_ref,
                pa_ref, pb_ref, wo_ref, g1_ref, b1_ref, h_ref, *, alpha):
    tm, d = x_ref.shape
    gw = d // GMLP_GROUPS
    vb = _layer_norm(vg_ref[0], lng_ref[...], lnb_ref[...]).astype(BF16)
    rows = []
    for n in range(tm // GMLP_CHUNK):
        cols = []
        for g in range(GMLP_GROUPS):
            vc = vb[n * GMLP_CHUNK:(n + 1) * GMLP_CHUNK, g * gw:(g + 1) * gw]
            cols.append(jnp.dot(ws_ref[g], vc, preferred_element_type=F32) + bs_ref[g])
        rows.append(jnp.concatenate(cols, axis=1))
    a = ug_ref[0] * jnp.concatenate(rows, axis=0)
    ma = jnp.dot(a.astype(BF16), pa_ref[...], preferred_element_type=F32)
    mb = jnp.dot(o_ref[...].astype(BF16), pb_ref[...], preferred_element_type=F32)
    m = sa_ref[0] * ma + sb_ref[0] * mb
    y = alpha * x_ref[...] + jnp.dot(m.astype(BF16), wo_ref[...], preferred_element_type=F32)
    h_ref[...] = _layer_norm(y, g1_ref[...], b1_ref[...])


def _mix(z, o2, x2, lng, lnb, ws_bf, bs_b, pa_bf, pb_bf, wo_bf, g1, b1, alpha, tm):
    n_tok, d = x2.shape
    full = lambda shape: pl.BlockSpec(shape, lambda i: (0,) * len(shape))
    seg = lambda s: pl.BlockSpec((1, tm, d), lambda i, s=s: (s, i, 0))
    tok = pl.BlockSpec((tm, d), lambda i: (i, 0))
    return pl.pallas_call(
        functools.partial(_mix_kernel, alpha=alpha),
        grid=(n_tok // tm,),
        in_specs=[seg(0), seg(1), seg(5), seg(6), tok, tok,
                  full((1, d)), full((1, d)), full(ws_bf.shape), full(bs_b.shape),
                  full((d, d)), full((d, d)), full((d, d)), full((1, d)), full((1, d))],
        out_specs=tok,
        out_shape=jax.ShapeDtypeStruct((n_tok, d), F32),
        compiler_params=pltpu.CompilerParams(
            dimension_semantics=("arbitrary",), vmem_limit_bytes=VMEM_LIMIT),
        name="mix",
    )(z, z, z, z, o2, x2, lng, lnb, ws_bf, bs_b, pa_bf, pb_bf, wo_bf, g1, b1)


def _staircase_pieces(topk):
    pieces = [("col", 0, topk)]
    for b in range(1, SUBLANES):
        pieces.append(("col", b, SUBLANES))
    pieces.append(("row", SUBLANES, topk - SUBLANES))
    return pieces


def _peer_select_kernel(h_ref, wq_ref, sk_ref, exp_ref, gate_ref, ts_scr, ti_scr, bs_scr, be_scr,
                        gt_scr, et_scr, *, n_heads, topk, row_scale):
    tm = h_ref.shape[0]
    nk, kh = sk_ref.shape[1], sk_ref.shape[2]
    q = jnp.dot(h_ref[...].astype(BF16), wq_ref[...], preferred_element_type=F32)
    qb = q.astype(BF16)
    key_iota = lax.broadcasted_iota(jnp.int32, (nk, tm), 0).astype(F32)

    pieces = _staircase_pieces(topk)
    flat_parts, valid_parts = [], []
    for kind, b, rows in pieces:
        r_iota = lax.broadcasted_iota(jnp.int32, (rows, tm), 0)
        if kind == "col":
            flat_parts.append(r_iota * topk + b)
            valid_parts.append(r_iota < topk // (b + 1))
        else:
            flat_parts.append(r_iota + b)
            valid_parts.append(r_iota >= 0)
    flat = jnp.concatenate(flat_parts, axis=0).astype(F32)
    valid = jnp.concatenate(valid_parts, axis=0)
    big = float(topk * topk)

    for hh in range(n_heads):
        for half in range(2):
            c0 = (hh * 2 + half) * kh
            s = lax.dot_general(sk_ref[half], qb[:, c0:c0 + kh], (((1,), (1,)), ((), ())),
                                preferred_element_type=F32)
            for r in range(topk):
                m = jnp.max(s, axis=0, keepdims=True)
                idx = jnp.min(jnp.where(s == m, key_iota, float(nk)), axis=0, keepdims=True)
                ts_scr[half, r:r + 1, :] = m
                ti_scr[half, r:r + 1, :] = idx
                s = jnp.where(key_iota == idx, -jnp.inf, s)
        t0, t1 = ts_scr[0], ts_scr[1]
        i0, i1 = ti_scr[0], ti_scr[1]
        cs, ce = [], []
        for kind, b, rows in pieces:
            if kind == "col":
                cs.append(t0[0:rows] + t1[b:b + 1])
                ce.append(i0[0:rows] * float(nk) + i1[b:b + 1])
            else:
                cs.append(t0[0:1] + t1[b:b + rows])
                ce.append(i0[0:1] * float(nk) + i1[b:b + rows])
        cand = jnp.where(valid, jnp.concatenate(cs, axis=0), -jnp.inf)
        cexp = jnp.concatenate(ce, axis=0)
        for r in range(topk):
            m = jnp.max(cand, axis=0, keepdims=True)
            fsel = jnp.min(jnp.where(cand == m, flat, big), axis=0, keepdims=True)
            hit = flat == fsel
            bs_scr[r:r + 1, :] = m
            be_scr[r:r + 1, :] = jnp.max(jnp.where(hit, cexp, -1.0), axis=0, keepdims=True)
            cand = jnp.where(hit, -jnp.inf, cand)
        best = bs_scr[...]
        ex = jnp.exp(best - jnp.max(best, axis=0, keepdims=True))
        gt_scr[hh * topk:(hh + 1) * topk, :] = ex / jnp.sum(ex, axis=0, keepdims=True)
        et_scr[hh * topk:(hh + 1) * topk, :] = be_scr[...] * float(row_scale)
    gate_ref[...] = gt_scr[...].T
    exp_ref[...] = et_scr[...].T.astype(jnp.int32)


def _peer_select(h2, wq_bf, sk_bf, tm):
    n_tok, d = h2.shape
    n_slots = PEER_HEADS * PEER_TOPK
    full = lambda shape: pl.BlockSpec(shape, lambda i: (0,) * len(shape))
    kern = functools.partial(_peer_select_kernel, n_heads=PEER_HEADS, topk=PEER_TOPK,
                             row_scale=ROWS_PER_EXPERT)
    return pl.pallas_call(
        kern,
        grid=(n_tok // tm,),
        in_specs=[pl.BlockSpec((tm, d), lambda i: (i, 0)), full(wq_bf.shape), full(sk_bf.shape)],
        out_specs=[pl.BlockSpec((tm, n_slots), lambda i: (i, 0)),
                   pl.BlockSpec((tm, n_slots), lambda i: (i, 0))],
        out_shape=[jax.ShapeDtypeStruct((n_tok, n_slots), jnp.int32),
                   jax.ShapeDtypeStruct((n_tok, n_slots), F32)],
        scratch_shapes=[pltpu.VMEM((2, PEER_TOPK, tm), F32), pltpu.VMEM((2, PEER_TOPK, tm), F32),
                        pltpu.VMEM((PEER_TOPK, tm), F32), pltpu.VMEM((PEER_TOPK, tm), F32),
                        pltpu.VMEM((n_slots, tm), F32), pltpu.VMEM((n_slots, tm), F32)],
        compiler_params=pltpu.CompilerParams(
            dimension_semantics=("arbitrary",), vmem_limit_bytes=VMEM_LIMIT),
        name="peer_select",
    )(h2, wq_bf, sk_bf)


ROWS_PER_EXPERT = 4
PEER_IDX_ROWS = 32
PEER_TOKENS_PER_STEP = 2 * PEER_IDX_ROWS


def _pack_table(tab):
    n_exp, d = tab.shape
    assert d == 2 * ROWS_PER_EXPERT * LANES
    bits = lax.bitcast_convert_type(tab.astype(BF16), jnp.uint16).astype(jnp.uint32)
    bits = bits.reshape(n_exp, ROWS_PER_EXPERT, 2, LANES)
    words = bits[:, :, 0, :] | (bits[:, :, 1, :] << 16)
    return words.reshape(n_exp * ROWS_PER_EXPERT, LANES)


def _idx_copy(idx_hbm, buf, sem, step, half):
    first = (step * 2 + half) * PEER_IDX_ROWS
    return pltpu.make_async_copy(idx_hbm.at[pl.ds(first, PEER_IDX_ROWS)], buf, sem)


def _peer_tokens(idx_hbm, idx_bufs, idx_sems, tbl_hbm, tbl_vmem, tbl_sem, tile_ref, token_fn):
    step = pl.program_id(0)
    n_steps = pl.num_programs(0)
    n_slots = idx_hbm.shape[1]

    @pl.when(step == 0)
    def _():
        for half in range(2):
            _idx_copy(idx_hbm, idx_bufs[half], idx_sems.at[half], 0, half).start()
        cp = pltpu.make_async_copy(tbl_hbm, tbl_vmem, tbl_sem)
        cp.start()
        cp.wait()

    for half in range(2):
        _idx_copy(idx_hbm, idx_bufs[half], idx_sems.at[half], step, half).wait()
        for r in range(PEER_IDX_ROWS):
            t = half * PEER_IDX_ROWS + r
            tile = tile_ref.at[t % 2]
            for k in range(n_slots):
                row = pl.multiple_of(idx_bufs[half][r, k], ROWS_PER_EXPERT)
                tile[k * ROWS_PER_EXPERT:(k + 1) * ROWS_PER_EXPERT, :] = tbl_vmem[pl.ds(row, ROWS_PER_EXPERT), :]
            token_fn(t, tile)

        @pl.when(step + 1 < n_steps)
        def _():
            _idx_copy(idx_hbm, idx_bufs[half], idx_sems.at[half], step + 1, half).start()


def _chunk_mask(width):
    r = lax.broadcasted_iota(jnp.int32, (SUBLANES, width), 0)
    j = lax.broadcasted_iota(jnp.int32, (SUBLANES, width), 1)
    return jnp.where(j % SUBLANES == r, 1.0, 0.0).astype(F32)


def _dot_exact_rhs(x, rhs_bf):
    rows = x.shape[0]
    p1 = x.astype(BF16)
    r1 = x - p1.astype(F32)
    p2 = r1.astype(BF16)
    p3 = (r1 - p2.astype(F32)).astype(BF16)
    res = jnp.dot(jnp.concatenate([p1, p2, p3], axis=0), rhs_bf, preferred_element_type=F32)
    return res[:rows] + res[rows:2 * rows] + res[2 * rows:]


def _peer_u_kernel(idx_hbm, x_ref, tbl_hbm, pair_ref, hid_ref, tbl_vmem, tile_ref, hsc_ref, idx_a, idx_b,
                   tbl_sem, idx_sems):
    mask = _chunk_mask(hsc_ref.shape[1])

    def token(t, tile):
        hi, lo = _split_bf16(x_ref[t])
        lhs = jnp.concatenate([hi, lo], axis=0)
        rhs = pltpu.bitcast(tile[...], BF16)
        out = lax.dot_general(lhs, rhs, (((1,), (1,)), ((), ())), preferred_element_type=F32)
        z = (out[0:SUBLANES] + out[SUBLANES:]) * mask
        hsc_ref[t:t + 1, :] = jnp.sum(z, axis=0, keepdims=True)

    _peer_tokens(idx_hbm, (idx_a, idx_b), idx_sems, tbl_hbm, tbl_vmem, tbl_sem, tile_ref, token)
    hid_ref[...] = _dot_exact_rhs(hsc_ref[...], pair_ref[...])


def _peer_v_kernel(idx_hbm, hid_ref, gate_ref, tbl_hbm, rep_ref, y_ref, tbl_vmem, tile_ref, wrep_ref, idx_a,
                   idx_b, tbl_sem, idx_sems):
    mask = _chunk_mask(wrep_ref.shape[1])
    w = gate_ref[...] * _gelu(hid_ref[...])
    wrep_ref[...] = _dot_exact_rhs(w, rep_ref[...])

    def token(t, tile):
        hi, lo = _split_bf16(wrep_ref[t:t + 1, :] * mask)
        lhs = jnp.concatenate([hi, lo], axis=0)
        rhs = pltpu.bitcast(tile[...], BF16)
        out = jnp.dot(lhs, rhs, preferred_element_type=F32)
        y_ref[t] = out[0:SUBLANES] + out[SUBLANES:]

    _peer_tokens(idx_hbm, (idx_a, idx_b), idx_sems, tbl_hbm, tbl_vmem, tbl_sem, tile_ref, token)


def _peer_scratch(tbl, tt, n_slots):
    assert tt == 2 * PEER_IDX_ROWS
    return [pltpu.VMEM(tbl.shape, jnp.uint32),
            pltpu.VMEM((2, n_slots * ROWS_PER_EXPERT, LANES), jnp.uint32),
            pltpu.VMEM((tt, n_slots * SUBLANES), F32),
            pltpu.SMEM((PEER_IDX_ROWS, n_slots), jnp.int32),
            pltpu.SMEM((PEER_IDX_ROWS, n_slots), jnp.int32),
            pltpu.SemaphoreType.DMA(()),
            pltpu.SemaphoreType.DMA((2,))]


def _peer_u(idx, h3, tbl, tt):
    n_tok, n_slots = idx.shape
    width = n_slots * SUBLANES
    pair = (np.arange(width)[:, None] // SUBLANES == np.arange(n_slots)[None, :]).astype(np.float32)
    return pl.pallas_call(
        _peer_u_kernel,
        grid=(n_tok // tt,),
        in_specs=[pl.BlockSpec(memory_space=pl.ANY),
                  pl.BlockSpec((tt, SUBLANES, LANES), lambda i: (i, 0, 0)),
                  pl.BlockSpec(memory_space=pl.ANY),
                  pl.BlockSpec((width, n_slots), lambda i: (0, 0))],
        out_specs=pl.BlockSpec((tt, n_slots), lambda i: (i, 0)),
        out_shape=jax.ShapeDtypeStruct((n_tok, n_slots), F32),
        scratch_shapes=_peer_scratch(tbl, tt, n_slots),
        compiler_params=pltpu.CompilerParams(
            dimension_semantics=("arbitrary",), vmem_limit_bytes=VMEM_LIMIT),
        name="peer_u",
    )(idx, h3, tbl, jnp.asarray(pair, dtype=BF16))


def _peer_v(idx, hid, gates, tbl, tt):
    n_tok, n_slots = idx.shape
    width = n_slots * SUBLANES
    rep = (np.arange(n_slots)[:, None] == np.arange(width)[None, :] // SUBLANES).astype(np.float32)
    tok = pl.BlockSpec((tt, n_slots), lambda i: (i, 0))
    return pl.pallas_call(
        _peer_v_kernel,
        grid=(n_tok // tt,),
        in_specs=[pl.BlockSpec(memory_space=pl.ANY),
                  tok, tok,
                  pl.BlockSpec(memory_space=pl.ANY),
                  pl.BlockSpec((n_slots, width), lambda i: (0, 0))],
        out_specs=pl.BlockSpec((tt, SUBLANES, LANES), lambda i: (i, 0, 0)),
        out_shape=jax.ShapeDtypeStruct((n_tok, SUBLANES, LANES), F32),
        scratch_shapes=_peer_scratch(tbl, tt, n_slots),
        compiler_params=pltpu.CompilerParams(
            dimension_semantics=("arbitrary",), vmem_limit_bytes=VMEM_LIMIT),
        name="peer_v",
    )(idx, hid, gates, tbl, jnp.asarray(rep, dtype=BF16))


def _final_kernel(h_ref, f_ref, p_ref, wple_ref, wpg_ref, bpg_ref, g_ref, b_ref, o_ref, *, alpha):
    h = h_ref[...]
    gate = jax.nn.sigmoid(jnp.dot(h.astype(BF16), wpg_ref[...], preferred_element_type=F32) + bpg_ref[...])
    e = jnp.dot(p_ref[...].astype(BF16), wple_ref[...], preferred_element_type=F32) * gate
    o_ref[...] = _layer_norm(alpha * h + f_ref[...] + e, g_ref[...], b_ref[...])


def _final(h2, f2, p2, wple_bf, wpg_bf, bpg, g2, b2, alpha, tm):
    n_tok, d = h2.shape
    pd = p2.shape[1]
    full = lambda shape: pl.BlockSpec(shape, lambda i: (0,) * len(shape))
    tok = pl.BlockSpec((tm, d), lambda i: (i, 0))
    return pl.pallas_call(
        functools.partial(_final_kernel, alpha=alpha),
        grid=(n_tok // tm,),
        in_specs=[tok, tok, pl.BlockSpec((tm, pd), lambda i: (i, 0)),
                  full((pd, d)), full((d, d)), full((1, d)), full((1, d)), full((1, d))],
        out_specs=tok,
        out_shape=jax.ShapeDtypeStruct((n_tok, d), F32),
        compiler_params=pltpu.CompilerParams(
            dimension_semantics=("arbitrary",), vmem_limit_bytes=VMEM_LIMIT),
        name="final",
    )(h2, f2, p2, wple_bf, wpg_bf, bpg, g2, b2)


def _token_tile(n_tok, want):
    tm = min(want, n_tok)
    assert n_tok % tm == 0
    return tm


def kernel(x, p, w_in, b_in, gmlp_ln_g, gmlp_ln_b, gmlp_w_s, gmlp_b_s, w_proj_a, w_proj_b, w_out,
           ln1_g, ln1_b, rel_bias, peer_w_q, peer_sub_keys, peer_u, peer_v, ple_w_proj, ple_w_gate,
           ple_b_gate, ln2_g, ln2_b):
    depth = w_in.shape[0]
    bsz, seq, d = x.shape
    n_tok = bsz * seq
    alpha = (2.0 * depth) ** 0.25
    assert d == GMLP_GROUPS * GMLP_CHUNK and seq % MOBA_BLOCK == 0
    row = lambda v: v.reshape(1, -1)
    bias = _bias_tiles(rel_bias, seq // MOBA_BLOCK, MOBA_BLOCK)
    x2 = x.reshape(n_tok, d)
    for i in range(depth):
        n_seg = w_in.shape[2] // d
        w_seg = w_in[i].astype(BF16).reshape(d, n_seg, d).transpose(1, 0, 2)
        z = _in_proj(x2, w_seg, b_in[i].reshape(n_seg, 1, d), _token_tile(n_tok, 256))
        o = _moba(z.reshape(z.shape[0], bsz, seq, d), bias, MOBA_HEADS)
        bs_b = jnp.broadcast_to(gmlp_b_s[i][:, :, None], gmlp_b_s[i].shape + (d // GMLP_GROUPS,))
        h2 = _mix(z, o.reshape(n_tok, d), x2, row(gmlp_ln_g[i]), row(gmlp_ln_b[i]),
                  jnp.tril(gmlp_w_s[i]).astype(BF16), bs_b,
                  w_proj_a[i].astype(BF16), w_proj_b[i].astype(BF16), w_out[i].astype(BF16),
                  row(ln1_g[i]), row(ln1_b[i]), alpha, _token_tile(n_tok, 256))
        experts, gates = _peer_select(h2, peer_w_q[i].astype(BF16), peer_sub_keys[i].astype(BF16),
                                      _token_tile(n_tok, 256))
        tt = _token_tile(n_tok, PEER_TOKENS_PER_STEP)
        hid = _peer_u(experts, h2.reshape(n_tok, SUBLANES, LANES), _pack_table(peer_u[i]), tt)
        f = _peer_v(experts, hid, gates, _pack_table(peer_v[i]), tt)
        x2 = _final(h2, f.reshape(n_tok, d), p[i].reshape(n_tok, -1), ple_w_proj[i].astype(BF16),
                    ple_w_gate[i].astype(BF16), row(ple_b_gate[i]), row(ln2_g[i]), row(ln2_b[i]),
                    alpha, _token_tile(n_tok, 512))
    return x2.reshape(bsz, seq, d)
```

```python
import functools
import math

import numpy as np
import jax
import jax.numpy as jnp
from jax import lax
from jax.experimental import pallas as pl
from jax.experimental.pallas import tpu as pltpu

F32 = jnp.float32
BF16 = jnp.bfloat16
HIGHEST = lax.Precision.HIGHEST

LANES = 128
SUBLANES = 8
LN_EPS = 1e-5
NEG_INF = -1e30

GMLP_GROUPS = 8
GMLP_CHUNK = 128
MOBA_HEADS = 8
MOBA_BLOCK = 256
MOBA_TOPK = 3
REL_BUCKETS = 32
REL_MAX_DISTANCE = 1024
PEER_HEADS = 8
PEER_TOPK = 16

VMEM_LIMIT = 56 * 1024 * 1024


def _layer_norm(x, g, b):
    mu = jnp.mean(x, axis=-1, keepdims=True)
    xc = x - mu
    var = jnp.mean(xc * xc, axis=-1, keepdims=True)
    return xc * lax.rsqrt(var + LN_EPS) * g + b


def _gelu(x):
    return 0.5 * x * (1.0 + lax.erf(x * math.sqrt(0.5)))


def _split_bf16(x):
    hi = x.astype(BF16)
    lo = (x - hi.astype(F32)).astype(BF16)
    return hi, lo


def _inproj_kernel(x_ref, w_ref, b_ref, o_ref):
    xb = x_ref[...].astype(BF16)
    for seg in range(w_ref.shape[0]):
        acc = jnp.dot(xb, w_ref[seg], preferred_element_type=F32) + b_ref[seg]
        if seg < 2:
            o_ref[seg] = _gelu(acc)
        elif seg < 5:
            o_ref[seg] = acc
        else:
            o_ref[seg] = jax.nn.sigmoid(acc)


def _in_proj(x2, w_seg, b_seg, tm):
    n_tok, d = x2.shape
    n_seg = w_seg.shape[0]
    return pl.pallas_call(
        _inproj_kernel,
        grid=(n_tok // tm,),
        in_specs=[
            pl.BlockSpec((tm, d), lambda i: (i, 0)),
            pl.BlockSpec((n_seg, d, d), lambda i: (0, 0, 0)),
            pl.BlockSpec((n_seg, 1, d), lambda i: (0, 0, 0)),
        ],
        out_specs=pl.BlockSpec((n_seg, tm, d), lambda i: (0, i, 0)),
        out_shape=jax.ShapeDtypeStruct((n_seg, n_tok, d), F32),
        compiler_params=pltpu.CompilerParams(
            dimension_semantics=("arbitrary",), vmem_limit_bytes=VMEM_LIMIT),
        name="in_proj",
    )(x2, w_seg, b_seg)


def _moba_kernel(q_ref, k_ref, v_ref, bias_ref, o_ref, kb_scr, vb_scr, kmean_scr, qaug_scr, *, scale):
    i = pl.program_id(2)
    seq, hd = kb_scr.shape
    blk = o_ref.shape[1]
    nb = seq // blk
    nbp = kmean_scr.shape[0]
    pair = 2 * blk

    @pl.when(i == 0)
    def _():
        kmean_scr[...] = jnp.zeros_like(kmean_scr)
        for n in range(nb):
            kf = k_ref[0, 0, n * blk:(n + 1) * blk, :]
            kmean_scr[n:n + 1, :] = jnp.mean(kf, axis=0, keepdims=True)
            kb_scr[n * blk:(n + 1) * blk, :] = kf.astype(BF16)
            vb_scr[n * blk:(n + 1) * blk, :] = v_ref[0, 0, n * blk:(n + 1) * blk, :].astype(BF16)
        for n in range(nb):
            q = q_ref[0, 0, n * blk:(n + 1) * blk, :]
            gate = lax.dot_general(kmean_scr[...], q, (((1,), (1,)), ((), ())),
                                   precision=HIGHEST, preferred_element_type=F32)
            n_iota = lax.broadcasted_iota(jnp.int32, gate.shape, 0)
            past = n_iota < n
            g = jnp.where(past, gate, NEG_INF)
            picked = jnp.zeros(gate.shape, F32)
            for _ in range(MOBA_TOPK):
                m = jnp.max(g, axis=0, keepdims=True)
                first = jnp.min(jnp.where(g == m, n_iota, nbp), axis=0, keepdims=True)
                hit = n_iota == first
                picked = jnp.where(hit, 1.0, picked)
                g = jnp.where(hit, -jnp.inf, g)
            allowed = jnp.logical_or(jnp.logical_and(picked > 0.5, past), n_iota == n)
            sel_neg = jnp.where(allowed, 0.0, NEG_INF)
            sel_pad = jnp.concatenate([sel_neg, jnp.full((LANES - nbp, blk), NEG_INF, F32)], axis=0)
            qaug_scr[n * blk:(n + 1) * blk, :] = jnp.concatenate(
                [(q * scale).astype(BF16), sel_pad.T.astype(BF16)], axis=1)

    q_aug = qaug_scr[pl.ds(pl.multiple_of(i * blk, blk), blk), :]

    row = lax.broadcasted_iota(jnp.int32, (blk, blk), 0)
    col = lax.broadcasted_iota(jnp.int32, (blk, blk), 1)
    tri = row >= col
    lane_blk = lax.broadcasted_iota(jnp.int32, (pair, LANES), 1)
    second = (lax.broadcasted_iota(jnp.int32, (pair, LANES), 0) >= blk).astype(jnp.int32)
    n_pairs = (i + 2) // 2

    def block_pair(it, carry):
        m_run, l_run, acc = carry
        n0 = 2 * (n_pairs - 1 - it)
        start = pl.multiple_of(n0 * blk, pair)
        onehot = jnp.where(lane_blk == n0 + second, 1.0, 0.0).astype(BF16)
        k_aug = jnp.concatenate([kb_scr[pl.ds(start, pair), :], onehot], axis=1)
        s = lax.dot_general(q_aug, k_aug, (((1,), (1,)), ((), ())), preferred_element_type=F32)
        s0 = s[:, :blk] + bias_ref[0, i - n0]
        s1 = s[:, blk:] + bias_ref[0, jnp.maximum(i - n0 - 1, 0)]
        s0 = jnp.where(jnp.logical_or(tri, n0 < i), s0, NEG_INF)
        s1 = jnp.where(jnp.logical_or(tri, n0 + 1 < i), s1, NEG_INF)
        s = jnp.concatenate([s0, s1], axis=1)
        m_new = jnp.maximum(m_run, jnp.max(s, axis=1, keepdims=True))
        corr = jnp.exp(m_run - m_new)
        p = jnp.exp(s - m_new)
        l_new = corr * l_run + jnp.sum(p, axis=1, keepdims=True)
        acc_new = corr * acc + jnp.dot(p.astype(BF16), vb_scr[pl.ds(start, pair), :],
                                       preferred_element_type=F32)
        return m_new, l_new, acc_new

    init = (jnp.full((blk, 1), NEG_INF, F32), jnp.zeros((blk, 1), F32), jnp.zeros((blk, hd), F32))
    _, l_fin, acc_fin = lax.fori_loop(0, n_pairs, block_pair, init)
    o_ref[0] = acc_fin / l_fin


def _t5_bucket(dist):
    n = jnp.maximum(dist, 0)
    max_exact = REL_BUCKETS // 2
    nf = jnp.maximum(n, max_exact).astype(jnp.float32)
    large = max_exact + (jnp.log(nf / max_exact) / math.log(REL_MAX_DISTANCE / max_exact)
                         * (REL_BUCKETS - max_exact)).astype(jnp.int32)
    large = jnp.minimum(large, REL_BUCKETS - 1)
    return jnp.where(n < max_exact, n, large)


def _bias_tiles(rel_bias, nb, blk):
    n_heads = rel_bias.shape[1]
    dist = jnp.arange(-(blk - 1), nb * blk, dtype=jnp.int32)
    bvec = rel_bias.T.astype(F32)[:, _t5_bucket(dist)]
    period = 2 * blk - 1
    win = jnp.stack([bvec[:, d * blk:d * blk + period] for d in range(nb)], axis=1)
    rev = win[..., ::-1]
    tiled = jnp.tile(rev, (1, 1, blk + 1))[..., :2 * blk * blk]
    rows = tiled.reshape(n_heads, nb, blk, 2 * blk)[..., :blk]
    return rows[:, :, ::-1, :]


def _moba(z4, bias, n_heads):
    _, bsz, seq, d = z4.shape
    hd = d // n_heads
    blk = MOBA_BLOCK
    nb = seq // blk
    assert nb <= LANES and nb % 2 == 0 and seq % blk == 0
    kern = functools.partial(_moba_kernel, scale=hd ** -0.5)
    return pl.pallas_call(
        kern,
        grid=(bsz, n_heads, nb),
        in_specs=[
            pl.BlockSpec((1, 1, seq, hd), lambda b, h, i: (2, b, 0, h)),
            pl.BlockSpec((1, 1, seq, hd), lambda b, h, i: (3, b, 0, h)),
            pl.BlockSpec((1, 1, seq, hd), lambda b, h, i: (4, b, 0, h)),
            pl.BlockSpec((1, nb, blk, blk), lambda b, h, i: (h, 0, 0, 0)),
        ],
        out_specs=pl.BlockSpec((1, blk, hd), lambda b, h, i: (b, i, h)),
        out_shape=jax.ShapeDtypeStruct((bsz, seq, d), F32),
        scratch_shapes=[
            pltpu.VMEM((seq, hd), BF16),
            pltpu.VMEM((seq, hd), BF16),
            pltpu.VMEM((-(-nb // SUBLANES) * SUBLANES, hd), F32),
            pltpu.VMEM((seq, hd + LANES), BF16),
        ],
        compiler_params=pltpu.CompilerParams(
            dimension_semantics=("arbitrary", "arbitrary", "arbitrary"),
            vmem_limit_bytes=VMEM_LIMIT),
        name="moba",
    )(z4, z4, z4, bias)


def _mix_kernel(ug_ref, vg_ref, sa_ref, sb_ref, o_ref, x_ref, lng_ref, lnb_ref, ws_ref, bs_ref,
                pa_ref, pb_ref, wo_ref, g1_ref, b1_ref, h_ref, *, alpha):
    tm, d = x_ref.shape
    gw = d // GMLP_GROUPS
    vb = _layer_norm(vg_ref[0], lng_ref[...], lnb_ref[...]).astype(BF16)
    rows = []
    for n in range(tm // GMLP_CHUNK):
        cols = []
        for g in range(GMLP_GROUPS):
            vc = vb[n * GMLP_CHUNK:(n + 1) * GMLP_CHUNK, g * gw:(g + 1) * gw]
            cols.append(jnp.dot(ws_ref[g], vc, preferred_element_type=F32) + bs_ref[g])
        rows.append(jnp.concatenate(cols, axis=1))
    a = ug_ref[0] * jnp.concatenate(rows, axis=0)
    ma = jnp.dot(a.astype(BF16), pa_ref[...], preferred_element_type=F32)
    mb = jnp.dot(o_ref[...].astype(BF16), pb_ref[...], preferred_element_type=F32)
    m = sa_ref[0] * ma + sb_ref[0] * mb
    y = alpha * x_ref[...] + jnp.dot(m.astype(BF16), wo_ref[...], preferred_element_type=F32)
    h_ref[...] = _layer_norm(y, g1_ref[...], b1_ref[...])


def _mix(z, o2, x2, lng, lnb, ws_bf, bs_b, pa_bf, pb_bf, wo_bf, g1, b1, alpha, tm):
    n_tok, d = x2.shape
    full = lambda shape: pl.BlockSpec(shape, lambda i: (0,) * len(shape))
    seg = lambda s: pl.BlockSpec((1, tm, d), lambda i, s=s: (s, i, 0))
    tok = pl.BlockSpec((tm, d), lambda i: (i, 0))
    return pl.pallas_call(
        functools.partial(_mix_kernel, alpha=alpha),
        grid=(n_tok // tm,),
        in_specs=[seg(0), seg(1), seg(5), seg(6), tok, tok,
                  full((1, d)), full((1, d)), full(ws_bf.shape), full(bs_b.shape),
                  full((d, d)), full((d, d)), full((d, d)), full((1, d)), full((1, d))],
        out_specs=tok,
        out_shape=jax.ShapeDtypeStruct((n_tok, d), F32),
        compiler_params=pltpu.CompilerParams(
            dimension_semantics=("arbitrary",), vmem_limit_bytes=VMEM_LIMIT),
        name="mix",
    )(z, z, z, z, o2, x2, lng, lnb, ws_bf, bs_b, pa_bf, pb_bf, wo_bf, g1, b1)


def _staircase_pieces(topk):
    pieces = [("col", 0, topk)]
    for b in range(1, SUBLANES):
        pieces.append(("col", b, SUBLANES))
    pieces.append(("row", SUBLANES, topk - SUBLANES))
    return pieces


def _peer_select_kernel(h_ref, wq_ref, sk_ref, exp_ref, gate_ref, ts_scr, ti_scr, bs_scr, be_scr,
                        gt_scr, et_scr, *, n_heads, topk, row_scale):
    tm = h_ref.shape[0]
    nk, kh = sk_ref.shape[1], sk_ref.shape[2]
    q = jnp.dot(h_ref[...].astype(BF16), wq_ref[...], preferred_element_type=F32)
    qb = q.astype(BF16)
    key_iota = lax.broadcasted_iota(jnp.int32, (nk, tm), 0).astype(F32)

    pieces = _staircase_pieces(topk)
    flat_parts, valid_parts = [], []
    for kind, b, rows in pieces:
        r_iota = lax.broadcasted_iota(jnp.int32, (rows, tm), 0)
        if kind == "col":
            flat_parts.append(r_iota * topk + b)
            valid_parts.append(r_iota < topk // (b + 1))
        else:
            flat_parts.append(r_iota + b)
            valid_parts.append(r_iota >= 0)
    flat = jnp.concatenate(flat_parts, axis=0).astype(F32)
    valid = jnp.concatenate(valid_parts, axis=0)
    big = float(topk * topk)

    for hh in range(n_heads):
        for half in range(2):
            c0 = (hh * 2 + half) * kh
            s = lax.dot_general(sk_ref[half], qb[:, c0:c0 + kh], (((1,), (1,)), ((), ())),
                                preferred_element_type=F32)
            for r in range(topk):
                m = jnp.max(s, axis=0, keepdims=True)
                idx = jnp.min(jnp.where(s == m, key_iota, float(nk)), axis=0, keepdims=True)
                ts_scr[half, r:r + 1, :] = m
                ti_scr[half, r:r + 1, :] = idx
                s = jnp.where(key_iota == idx, -jnp.inf, s)
        t0, t1 = ts_scr[0], ts_scr[1]
        i0, i1 = ti_scr[0], ti_scr[1]
        cs, ce = [], []
        for kind, b, rows in pieces:
            if kind == "col":
                cs.append(t0[0:rows] + t1[b:b + 1])
                ce.append(i0[0:rows] * float(nk) + i1[b:b + 1])
            else:
                cs.append(t0[0:1] + t1[b:b + rows])
                ce.append(i0[0:1] * float(nk) + i1[b:b + rows])
        cand = jnp.where(valid, jnp.concatenate(cs, axis=0), -jnp.inf)
        cexp = jnp.concatenate(ce, axis=0)
        for r in range(topk):
            m = jnp.max(cand, axis=0, keepdims=True)
            fsel = jnp.min(jnp.where(cand == m, flat, big), axis=0, keepdims=True)
            hit = flat == fsel
            bs_scr[r:r + 1, :] = m
            be_scr[r:r + 1, :] = jnp.max(jnp.where(hit, cexp, -1.0), axis=0, keepdims=True)
            cand = jnp.where(hit, -jnp.inf, cand)
        best = bs_scr[...]
        ex = jnp.exp(best - jnp.max(best, axis=0, keepdims=True))
        gt_scr[hh * topk:(hh + 1) * topk, :] = ex / jnp.sum(ex, axis=0, keepdims=True)
        et_scr[hh * topk:(hh + 1) * topk, :] = be_scr[...] * float(row_scale)
    gate_ref[...] = gt_scr[...].T
    exp_ref[...] = et_scr[...].T.astype(jnp.int32)


def _peer_select(h2, wq_bf, sk_bf, tm):
    n_tok, d = h2.shape
    n_slots = PEER_HEADS * PEER_TOPK
    full = lambda shape: pl.BlockSpec(shape, lambda i: (0,) * len(shape))
    kern = functools.partial(_peer_select_kernel, n_heads=PEER_HEADS, topk=PEER_TOPK,
                             row_scale=ROWS_PER_EXPERT)
    return pl.pallas_call(
        kern,
        grid=(n_tok // tm,),
        in_specs=[pl.BlockSpec((tm, d), lambda i: (i, 0)), full(wq_bf.shape), full(sk_bf.shape)],
        out_specs=[pl.BlockSpec((tm, n_slots), lambda i: (i, 0)),
                   pl.BlockSpec((tm, n_slots), lambda i: (i, 0))],
        out_shape=[jax.ShapeDtypeStruct((n_tok, n_slots), jnp.int32),
                   jax.ShapeDtypeStruct((n_tok, n_slots), F32)],
        scratch_shapes=[pltpu.VMEM((2, PEER_TOPK, tm), F32), pltpu.VMEM((2, PEER_TOPK, tm), F32),
                        pltpu.VMEM((PEER_TOPK, tm), F32), pltpu.VMEM((PEER_TOPK, tm), F32),
                        pltpu.VMEM((n_slots, tm), F32), pltpu.VMEM((n_slots, tm), F32)],
        compiler_params=pltpu.CompilerParams(
            dimension_semantics=("arbitrary",), vmem_limit_bytes=VMEM_LIMIT),
        name="peer_select",
    )(h2, wq_bf, sk_bf)


ROWS_PER_EXPERT = 4
PEER_IDX_ROWS = 32
PEER_TOKENS_PER_STEP = 2 * PEER_IDX_ROWS


def _pack_table(tab):
    n_exp, d = tab.shape
    assert d == 2 * ROWS_PER_EXPERT * LANES
    bits = lax.bitcast_convert_type(tab.astype(BF16), jnp.uint16).astype(jnp.uint32)
    bits = bits.reshape(n_exp, ROWS_PER_EXPERT, 2, LANES)
    words = bits[:, :, 0, :] | (bits[:, :, 1, :] << 16)
    return words.reshape(n_exp * ROWS_PER_EXPERT, LANES)


def _idx_copy(idx_hbm, buf, sem, step, half):
    first = (step * 2 + half) * PEER_IDX_ROWS
    return pltpu.make_async_copy(idx_hbm.at[pl.ds(first, PEER_IDX_ROWS)], buf, sem)


def _peer_tokens(idx_hbm, idx_bufs, idx_sems, tbl_hbm, tbl_vmem, tbl_sem, tile_ref, token_fn):
    step = pl.program_id(0)
    n_steps = pl.num_programs(0)
    n_slots = idx_hbm.shape[1]

    @pl.when(step == 0)
    def _():
        for half in range(2):
            _idx_copy(idx_hbm, idx_bufs[half], idx_sems.at[half], 0, half).start()
        cp = pltpu.make_async_copy(tbl_hbm, tbl_vmem, tbl_sem)
        cp.start()
        cp.wait()

    for half in range(2):
        _idx_copy(idx_hbm, idx_bufs[half], idx_sems.at[half], step, half).wait()
        for r in range(PEER_IDX_ROWS):
            t = half * PEER_IDX_ROWS + r
            tile = tile_ref.at[t % 2]
            for k in range(n_slots):
                row = pl.multiple_of(idx_bufs[half][r, k], ROWS_PER_EXPERT)
                tile[k * ROWS_PER_EXPERT:(k + 1) * ROWS_PER_EXPERT, :] = tbl_vmem[pl.ds(row, ROWS_PER_EXPERT), :]
            token_fn(t, tile)

        @pl.when(step + 1 < n_steps)
        def _():
            _idx_copy(idx_hbm, idx_bufs[half], idx_sems.at[half], step + 1, half).start()


def _chunk_mask(width):
    r = lax.broadcasted_iota(jnp.int32, (SUBLANES, width), 0)
    j = lax.broadcasted_iota(jnp.int32, (SUBLANES, width), 1)
    return jnp.where(j % SUBLANES == r, 1.0, 0.0).astype(F32)


def _dot_exact_rhs(x, rhs_bf):
    rows = x.shape[0]
    p1 = x.astype(BF16)
    r1 = x - p1.astype(F32)
    p2 = r1.astype(BF16)
    p3 = (r1 - p2.astype(F32)).astype(BF16)
    res = jnp.dot(jnp.concatenate([p1, p2, p3], axis=0), rhs_bf, preferred_element_type=F32)
    return res[:rows] + res[rows:2 * rows] + res[2 * rows:]


def _peer_u_kernel(idx_hbm, x_ref, tbl_hbm, pair_ref, hid_ref, tbl_vmem, tile_ref, hsc_ref, idx_a, idx_b,
                   tbl_sem, idx_sems):
    mask = _chunk_mask(hsc_ref.shape[1])

    def token(t, tile):
        hi, lo = _split_bf16(x_ref[t])
        lhs = jnp.concatenate([hi, lo], axis=0)
        rhs = pltpu.bitcast(tile[...], BF16)
        out = lax.dot_general(lhs, rhs, (((1,), (1,)), ((), ())), preferred_element_type=F32)
        z = (out[0:SUBLANES] + out[SUBLANES:]) * mask
        hsc_ref[t:t + 1, :] = jnp.sum(z, axis=0, keepdims=True)

    _peer_tokens(idx_hbm, (idx_a, idx_b), idx_sems, tbl_hbm, tbl_vmem, tbl_sem, tile_ref, token)
    hid_ref[...] = _dot_exact_rhs(hsc_ref[...], pair_ref[...])


def _peer_v_kernel(idx_hbm, hid_ref, gate_ref, tbl_hbm, rep_ref, y_ref, tbl_vmem, tile_ref, wrep_ref, idx_a,
                   idx_b, tbl_sem, idx_sems):
    mask = _chunk_mask(wrep_ref.shape[1])
    w = gate_ref[...] * _gelu(hid_ref[...])
    wrep_ref[...] = _dot_exact_rhs(w, rep_ref[...])

    def token(t, tile):
        hi, lo = _split_bf16(wrep_ref[t:t + 1, :] * mask)
        lhs = jnp.concatenate([hi, lo], axis=0)
        rhs = pltpu.bitcast(tile[...], BF16)
        out = jnp.dot(lhs, rhs, preferred_element_type=F32)
        y_ref[t] = out[0:SUBLANES] + out[SUBLANES:]

    _peer_tokens(idx_hbm, (idx_a, idx_b), idx_sems, tbl_hbm, tbl_vmem, tbl_sem, tile_ref, token)


def _peer_scratch(tbl, tt, n_slots):
    assert tt == 2 * PEER_IDX_ROWS
    return [pltpu.VMEM(tbl.shape, jnp.uint32),
            pltpu.VMEM((2, n_slots * ROWS_PER_EXPERT, LANES), jnp.uint32),
            pltpu.VMEM((tt, n_slots * SUBLANES), F32),
            pltpu.SMEM((PEER_IDX_ROWS, n_slots), jnp.int32),
            pltpu.SMEM((PEER_IDX_ROWS, n_slots), jnp.int32),
            pltpu.SemaphoreType.DMA(()),
            pltpu.SemaphoreType.DMA((2,))]


def _peer_u(idx, h3, tbl, tt):
    n_tok, n_slots = idx.shape
    width = n_slots * SUBLANES
    pair = (np.arange(width)[:, None] // SUBLANES == np.arange(n_slots)[None, :]).astype(np.float32)
    return pl.pallas_call(
        _peer_u_kernel,
        grid=(n_tok // tt,),
        in_specs=[pl.BlockSpec(memory_space=pl.ANY),
                  pl.BlockSpec((tt, SUBLANES, LANES), lambda i: (i, 0, 0)),
                  pl.BlockSpec(memory_space=pl.ANY),
                  pl.BlockSpec((width, n_slots), lambda i: (0, 0))],
        out_specs=pl.BlockSpec((tt, n_slots), lambda i: (i, 0)),
        out_shape=jax.ShapeDtypeStruct((n_tok, n_slots), F32),
        scratch_shapes=_peer_scratch(tbl, tt, n_slots),
        compiler_params=pltpu.CompilerParams(
            dimension_semantics=("arbitrary",), vmem_limit_bytes=VMEM_LIMIT),
        name="peer_u",
    )(idx, h3, tbl, jnp.asarray(pair, dtype=BF16))


def _peer_v(idx, hid, gates, tbl, tt):
    n_tok, n_slots = idx.shape
    width = n_slots * SUBLANES
    rep = (np.arange(n_slots)[:, None] == np.arange(width)[None, :] // SUBLANES).astype(np.float32)
    tok = pl.BlockSpec((tt, n_slots), lambda i: (i, 0))
    return pl.pallas_call(
        _peer_v_kernel,
        grid=(n_tok // tt,),
        in_specs=[pl.BlockSpec(memory_space=pl.ANY),
                  tok, tok,
                  pl.BlockSpec(memory_space=pl.ANY),
                  pl.BlockSpec((n_slots, width), lambda i: (0, 0))],
        out_specs=pl.BlockSpec((tt, SUBLANES, LANES), lambda i: (i, 0, 0)),
        out_shape=jax.ShapeDtypeStruct((n_tok, SUBLANES, LANES), F32),
        scratch_shapes=_peer_scratch(tbl, tt, n_slots),
        compiler_params=pltpu.CompilerParams(
            dimension_semantics=("arbitrary",), vmem_limit_bytes=VMEM_LIMIT),
        name="peer_v",
    )(idx, hid, gates, tbl, jnp.asarray(rep, dtype=BF16))


def _final_kernel(h_ref, f_ref, p_ref, wple_ref, wpg_ref, bpg_ref, g_ref, b_ref, o_ref, *, alpha):
    h = h_ref[...]
    gate = jax.nn.sigmoid(jnp.dot(h.astype(BF16), wpg_ref[...], preferred_element_type=F32) + bpg_ref[...])
    e = jnp.dot(p_ref[...].astype(BF16), wple_ref[...], preferred_element_type=F32) * gate
    o_ref[...] = _layer_norm(alpha * h + f_ref[...] + e, g_ref[...], b_ref[...])


def _final(h2, f2, p2, wple_bf, wpg_bf, bpg, g2, b2, alpha, tm):
    n_tok, d = h2.shape
    pd = p2.shape[1]
    full = lambda shape: pl.BlockSpec(shape, lambda i: (0,) * len(shape))
    tok = pl.BlockSpec((tm, d), lambda i: (i, 0))
    return pl.pallas_call(
        functools.partial(_final_kernel, alpha=alpha),
        grid=(n_tok // tm,),
        in_specs=[tok, tok, pl.BlockSpec((tm, pd), lambda i: (i, 0)),
                  full((pd, d)), full((d, d)), full((1, d)), full((1, d)), full((1, d))],
        out_specs=tok,
        out_shape=jax.ShapeDtypeStruct((n_tok, d), F32),
        compiler_params=pltpu.CompilerParams(
            dimension_semantics=("arbitrary",), vmem_limit_bytes=VMEM_LIMIT),
        name="final",
    )(h2, f2, p2, wple_bf, wpg_bf, bpg, g2, b2)


def _token_tile(n_tok, want):
    tm = min(want, n_tok)
    assert n_tok % tm == 0
    return tm


def kernel(x, p, w_in, b_in, gmlp_ln_g, gmlp_ln_b, gmlp_w_s, gmlp_b_s, w_proj_a, w_proj_b, w_out,
           ln1_g, ln1_b, rel_bias, peer_w_q, peer_sub_keys, peer_u, peer_v, ple_w_proj, ple_w_gate,
           ple_b_gate, ln2_g, ln2_b):
    depth = w_in.shape[0]
    bsz, seq, d = x.shape
    n_tok = bsz * seq
    alpha = (2.0 * depth) ** 0.25
    assert d == GMLP_GROUPS * GMLP_CHUNK and seq % MOBA_BLOCK == 0
    row = lambda v: v.reshape(1, -1)
    bias = _bias_tiles(rel_bias, seq // MOBA_BLOCK, MOBA_BLOCK)
    x2 = x.reshape(n_tok, d)
    for i in range(depth):
        n_seg = w_in.shape[2] // d
        w_seg = w_in[i].astype(BF16).reshape(d, n_seg, d).transpose(1, 0, 2)
        z = _in_proj(x2, w_seg, b_in[i].reshape(n_seg, 1, d), _token_tile(n_tok, 256))
        o = _moba(z.reshape(z.shape[0], bsz, seq, d), bias, MOBA_HEADS)
        bs_b = jnp.broadcast_to(gmlp_b_s[i][:, :, None], gmlp_b_s[i].shape + (d // GMLP_GROUPS,))
        h2 = _mix(z, o.reshape(n_tok, d), x2, row(gmlp_ln_g[i]), row(gmlp_ln_b[i]),
                  jnp.tril(gmlp_w_s[i]).astype(BF16), bs_b,
                  w_proj_a[i].astype(BF16), w_proj_b[i].astype(BF16), w_out[i].astype(BF16),
                  row(ln1_g[i]), row(ln1_b[i]), alpha, _token_tile(n_tok, 256))
        experts, gates = _peer_select(h2, peer_w_q[i].astype(BF16), peer_sub_keys[i].astype(BF16),
                                      _token_tile(n_tok, 256))
        tt = _token_tile(n_tok, PEER_TOKENS_PER_STEP)
        hid = _peer_u(experts, h2.reshape(n_tok, SUBLANES, LANES), _pack_table(peer_u[i]), tt)
        f = _peer_v(experts, hid, gates, _pack_table(peer_v[i]), tt)
        x2 = _final(h2, f.reshape(n_tok, d), p[i].reshape(n_tok, -1), ple_w_proj[i].astype(BF16),
                    ple_w_gate[i].astype(BF16), row(ple_b_gate[i]), row(ln2_g[i]), row(ln2_b[i]),
                    alpha, _token_tile(n_tok, 512))
    return x2.reshape(bsz, seq, d)
```
